```python
import math
import jax, jax.numpy as jnp
from jax import lax
import numpy as np

D_MODEL = 2048
BATCH = 4
SEQ = 8192
DEPTH = 1

PLE_DIM = 256
HEAD_DIM = 128
N_FOX_HEADS = 8
N_DIFF_HEADS = 8
DIFF_QK_DIM = HEAD_DIM // 2
FOX_WIDTH = N_FOX_HEADS * HEAD_DIM
DIFF_WIDTH = N_DIFF_HEADS * HEAD_DIM
MIX_WIDTH = FOX_WIDTH + DIFF_WIDTH
IN_WIDTH = 4 * FOX_WIDTH + 4 * DIFF_WIDTH + N_FOX_HEADS
N_BUCKETS = 32
MAX_DISTANCE = 128
Q_BLOCK = 128
EPS = 1e-6

kernel_name = "hymba_fox_diffattn_ple_layer"


def rms_norm(x, gain, eps=EPS):
    xf = x.astype(jnp.float32)
    y = xf * lax.rsqrt(jnp.mean(xf * xf, axis=-1, keepdims=True) + eps)
    return (y * gain.astype(jnp.float32)).astype(x.dtype)


def t5_causal_bucket(n):
    max_exact = N_BUCKETS // 2
    is_small = n < max_exact
    nf = jnp.maximum(n, 1).astype(jnp.float32)
    large = max_exact + (jnp.log(nf / max_exact) / math.log(MAX_DISTANCE / max_exact)
                         * (N_BUCKETS - max_exact)).astype(jnp.int32)
    large = jnp.minimum(large, N_BUCKETS - 1)
    return jnp.where(is_small, n, large)


def to_heads(t, n_heads):
    b, s, _ = t.shape
    return t.reshape(b, s, n_heads, -1).transpose(0, 2, 1, 3)


def merge_heads(t):
    b, h, s, d = t.shape
    return t.transpose(0, 2, 1, 3).reshape(b, s, h * d)


def to_blocks(t):
    b, h, s = t.shape[:3]
    nb = s // Q_BLOCK
    t = t.reshape((b, h, nb, Q_BLOCK) + t.shape[3:])
    return jnp.moveaxis(t, 2, 0)


def from_blocks(t):
    nb, b, h, q, d = t.shape
    return jnp.moveaxis(t, 0, 2).reshape(b, h, nb * q, d)


def fox_attention(q, k, v, log_f):
    s_len, dh = q.shape[2], q.shape[3]
    scale = dh ** -0.5
    c = jnp.cumsum(log_f, axis=-1)
    kf = k.astype(jnp.float32)
    vf = v.astype(jnp.float32)
    k_pos = jnp.arange(s_len)
    nb = s_len // Q_BLOCK

    def block(args):
        qb, cb, i = args
        q_pos = i * Q_BLOCK + jnp.arange(Q_BLOCK)
        s = jnp.einsum('bhqd,bhkd->bhqk', qb.astype(jnp.float32), kf) * scale
        s = s + cb[..., :, None] - c[..., None, :]
        causal = k_pos[None, :] <= q_pos[:, None]
        s = jnp.where(causal, s, -jnp.inf)
        pr = jax.nn.softmax(s, axis=-1)
        return jnp.einsum('bhqk,bhkd->bhqd', pr, vf)

    out = lax.map(block, (to_blocks(q), to_blocks(c), jnp.arange(nb)))
    return from_blocks(out)


def diff_attention(q1, q2, k1, k2, v, bias_by_dist, lam):
    s_len, dq = q1.shape[2], q1.shape[3]
    scale = dq ** -0.5
    k1f = k1.astype(jnp.float32)
    k2f = k2.astype(jnp.float32)
    vf = v.astype(jnp.float32)
    k_pos = jnp.arange(s_len)
    nb = s_len // Q_BLOCK

    def block(args):
        q1b, q2b, i = args
        q_pos = i * Q_BLOCK + jnp.arange(Q_BLOCK)
        dist = q_pos[:, None] - k_pos[None, :]
        causal = dist >= 0
        bias = bias_by_dist[:, jnp.clip(dist, 0, s_len - 1)][None]

        def softmax_map(qb, kf):
            s = jnp.einsum('bhqd,bhkd->bhqk', qb.astype(jnp.float32), kf) * scale + bias
            return jax.nn.softmax(jnp.where(causal, s, -jnp.inf), axis=-1)

        a = softmax_map(q1b, k1f) - lam * softmax_map(q2b, k2f)
        return jnp.einsum('bhqk,bhkd->bhqd', a, vf)

    out = lax.map(block, (to_blocks(q1), to_blocks(q2), jnp.arange(nb)))
    return from_blocks(out)


def setup_inputs(seed: int = 0) -> dict:
    key = jax.random.key(seed)
    ks = jax.random.split(key, 20)
    f32 = jnp.float32
    nrm = lambda k, shape, s: (jax.random.normal(k, shape, f32) * s)
    return {
        "x": nrm(ks[0], (BATCH, SEQ, D_MODEL), 1.0),
        "p": nrm(ks[1], (DEPTH, BATCH, SEQ, PLE_DIM), 1.0),
        "attn_norm": 1.0 + nrm(ks[2], (DEPTH, D_MODEL), 0.02),
        "w_in": nrm(ks[3], (DEPTH, D_MODEL, IN_WIDTH), D_MODEL ** -0.5),
        "b_forget": 2.0 + nrm(ks[4], (DEPTH, N_FOX_HEADS), 0.5),
        "fox_q_norm": 1.0 + nrm(ks[5], (DEPTH, HEAD_DIM), 0.02),
        "fox_k_norm": 1.0 + nrm(ks[6], (DEPTH, HEAD_DIM), 0.02),
        "diff_q_norm": 1.0 + nrm(ks[7], (DEPTH, DIFF_QK_DIM), 0.02),
        "diff_k_norm": 1.0 + nrm(ks[8], (DEPTH, DIFF_QK_DIM), 0.02),
        "lambda_q1": nrm(ks[9], (DEPTH, DIFF_QK_DIM), 0.1),
        "lambda_k1": nrm(ks[10], (DEPTH, DIFF_QK_DIM), 0.1),
        "lambda_q2": nrm(ks[11], (DEPTH, DIFF_QK_DIM), 0.1),
        "lambda_k2": nrm(ks[12], (DEPTH, DIFF_QK_DIM), 0.1),
        "diff_out_norm": 1.0 + nrm(ks[13], (DEPTH, HEAD_DIM), 0.02),
        "w_out": nrm(ks[14], (DEPTH, MIX_WIDTH, D_MODEL), MIX_WIDTH ** -0.5),
        "rel_bias": nrm(ks[15], (N_BUCKETS, N_DIFF_HEADS), 0.5),
        "ple_proj": nrm(ks[16], (DEPTH, PLE_DIM, D_MODEL), 0.5 * PLE_DIM ** -0.5),
        "ple_gate_norm": 1.0 + nrm(ks[17], (DEPTH, D_MODEL), 0.02),
        "ple_gate": nrm(ks[18], (DEPTH, D_MODEL, D_MODEL), D_MODEL ** -0.5),
    }


def reference(x, p, attn_norm, w_in, b_forget, fox_q_norm, fox_k_norm, diff_q_norm,
              diff_k_norm, lambda_q1, lambda_k1, lambda_q2, lambda_k2, diff_out_norm,
              w_out, rel_bias, ple_proj, ple_gate_norm, ple_gate):
    b, s_len, _ = x.shape
    bias_by_dist = rel_bias.astype(jnp.float32)[t5_causal_bucket(jnp.arange(s_len))].T
    split_at = [FOX_WIDTH * j for j in range(1, 5)] + \
               [4 * FOX_WIDTH + DIFF_WIDTH * j for j in range(1, 5)]
    h = x
    for i in range(DEPTH):
        u = rms_norm(h, attn_norm[i])
        proj = jnp.einsum('bsd,de->bse', u, w_in[i])
        fq, fk, fv, fz, dq, dk, dv, dz, f_logit = jnp.split(proj, split_at, axis=-1)

        q_a = rms_norm(to_heads(fq, N_FOX_HEADS), fox_q_norm[i])
        k_a = rms_norm(to_heads(fk, N_FOX_HEADS), fox_k_norm[i])
        v_a = to_heads(fv, N_FOX_HEADS)
        log_f = jax.nn.log_sigmoid(
            (f_logit + b_forget[i]).astype(jnp.float32)).transpose(0, 2, 1)
        o_a = merge_heads(fox_attention(q_a, k_a, v_a, log_f))
        o_a = o_a * jax.nn.silu(fz.astype(jnp.float32))

        lambda_init = 0.8 - 0.6 * math.exp(-0.3 * i)
        lam = (jnp.exp(jnp.sum(lambda_q1[i].astype(jnp.float32) * lambda_k1[i].astype(jnp.float32)))
               - jnp.exp(jnp.sum(lambda_q2[i].astype(jnp.float32) * lambda_k2[i].astype(jnp.float32)))
               + lambda_init)
        q_b = rms_norm(dq.reshape(b, s_len, N_DIFF_HEADS, 2, DIFF_QK_DIM), diff_q_norm[i])
        k_b = rms_norm(dk.reshape(b, s_len, N_DIFF_HEADS, 2, DIFF_QK_DIM), diff_k_norm[i])
        q1 = q_b[:, :, :, 0].transpose(0, 2, 1, 3)
        q2 = q_b[:, :, :, 1].transpose(0, 2, 1, 3)
        k1 = k_b[:, :, :, 0].transpose(0, 2, 1, 3)
        k2 = k_b[:, :, :, 1].transpose(0, 2, 1, 3)
        v_b = to_heads(dv, N_DIFF_HEADS)
        o_b = diff_attention(q1, q2, k1, k2, v_b, bias_by_dist, lam)
        o_b = rms_norm(o_b, diff_out_norm[i]) * (1.0 - lambda_init)
        o_b = merge_heads(o_b) * jax.nn.silu(dz.astype(jnp.float32))

        mix = jnp.concatenate([o_a, o_b], axis=-1).astype(h.dtype)
        h = h + jnp.einsum('bse,ed->bsd', mix, w_out[i])

        e = jnp.einsum('bsk,kd->bsd', p[i], ple_proj[i])
        g = jax.nn.sigmoid(jnp.einsum('bsd,de->bse', rms_norm(h, ple_gate_norm[i]), ple_gate[i]))
        h = h + (g * e).astype(h.dtype)
    return h
```

```python
import functools
import math

import jax
import jax.numpy as jnp
from jax import lax
from jax.experimental import pallas as pl
from jax.experimental.pallas import tpu as pltpu

F32 = jnp.float32
BF16 = jnp.bfloat16

HEAD_DIM = 128
N_HEADS = 8
DIFF_QK = HEAD_DIM // 2
GROUP_WIDTH = N_HEADS * HEAD_DIM
N_BUCKETS = 32
MAX_DISTANCE = 128
EPS = 1e-6
LOG2E = math.log2(math.e)
NEG = -1e30
LANES = 128
AUG = 2 * HEAD_DIM
VMEM_LIMIT = 56 * 1024 * 1024


def _cparams(sem):
    return pltpu.CompilerParams(dimension_semantics=sem, vmem_limit_bytes=VMEM_LIMIT)


def _split3(v):
    p0 = v.astype(BF16).astype(F32)
    r1 = v - p0
    p1 = r1.astype(BF16).astype(F32)
    p2 = (r1 - p1).astype(BF16).astype(F32)
    return p0, p1, p2


def _norm_forget_kernel(x_ref, g_ref, wf_ref, bf_ref, u_ref, c_ref, carry_ref, *, tiles_per_seq):
    i = pl.program_id(0)
    x = x_ref[...]
    tm = x.shape[0]
    ms = jnp.mean(x * x, axis=-1, keepdims=True)
    u = (x * lax.rsqrt(ms + EPS) * g_ref[...]).astype(BF16)
    u_ref[...] = u

    z = jnp.dot(u, wf_ref[...], preferred_element_type=F32) + bf_ref[...]
    lf = (jnp.minimum(z, 0.0) - jnp.log(1.0 + jnp.exp(-jnp.abs(z)))) * LOG2E

    @pl.when(i % tiles_per_seq == 0)
    def _():
        carry_ref[...] = jnp.zeros_like(carry_ref)

    row = lax.broadcasted_iota(jnp.int32, (tm, tm), 0)
    col = lax.broadcasted_iota(jnp.int32, (tm, tm), 1)
    tri = jnp.where(col <= row, 1.0, 0.0).astype(BF16)
    c = carry_ref[...]
    for part in _split3(lf):
        c = c + jnp.dot(tri, part.astype(BF16), preferred_element_type=F32)
    c_ref[...] = c
    carry_ref[...] = c[tm - 1:tm, :]


def _norm_forget(x2, gain, w_f, b_f, seq, tm):
    m, d = x2.shape
    kern = functools.partial(_norm_forget_kernel, tiles_per_seq=seq // tm)
    return pl.pallas_call(
        kern,
        grid=(m // tm,),
        in_specs=[
            pl.BlockSpec((tm, d), lambda i: (i, 0)),
            pl.BlockSpec((1, d), lambda i: (0, 0)),
            pl.BlockSpec((d, LANES), lambda i: (0, 0)),
            pl.BlockSpec((1, LANES), lambda i: (0, 0)),
        ],
        out_specs=[
            pl.BlockSpec((tm, d), lambda i: (i, 0)),
            pl.BlockSpec((tm, LANES), lambda i: (i, 0)),
        ],
        out_shape=[
            jax.ShapeDtypeStruct((m, d), BF16),
            jax.ShapeDtypeStruct((m, LANES), F32),
        ],
        scratch_shapes=[pltpu.VMEM((1, LANES), F32)],
        compiler_params=_cparams(("arbitrary",)),
        name="norm_forget",
    )(x2, gain, w_f, b_f)


def _proj_plain_kernel(u_ref, w_ref, o_ref):
    y = jnp.dot(u_ref[...], w_ref[...], preferred_element_type=F32)
    o_ref[...] = y.astype(o_ref.dtype)


def _proj_silu_kernel(u_ref, w_ref, o_ref):
    y = jnp.dot(u_ref[...], w_ref[...], preferred_element_type=F32)
    o_ref[...] = (y * (1.0 / (1.0 + jnp.exp(-y)))).astype(o_ref.dtype)


def _proj_fox_kernel(u_ref, w_ref, g_ref, c_ref, o_ref, *, is_q, scale):
    y = jnp.dot(u_ref[...], w_ref[...], preferred_element_type=F32)
    tm = y.shape[0]
    lane = lax.broadcasted_iota(jnp.int32, (tm, LANES), 1)
    gain = g_ref[...] * scale
    c_all = c_ref[...]
    for h in range(N_HEADS):
        yh = y[:, h * HEAD_DIM:(h + 1) * HEAD_DIM]
        ms = jnp.mean(yh * yh, axis=-1, keepdims=True)
        o_ref[:, h * AUG:h * AUG + HEAD_DIM] = (yh * lax.rsqrt(ms + EPS) * gain).astype(BF16)
        ch = jnp.broadcast_to(c_all[:, h:h + 1], (tm, LANES))
        if not is_q:
            ch = -ch
        p0, p1, p2 = _split3(ch)
        first = 0 if is_q else 3
        ones_at = 3 if is_q else 0
        aug = jnp.where(lane == first, p0, jnp.where(lane == first + 1, p1, jnp.where(lane == first + 2, p2, 0.0)))
        aug = jnp.where((lane >= ones_at) & (lane < ones_at + 3), 1.0, aug)
        o_ref[:, h * AUG + HEAD_DIM:(h + 1) * AUG] = aug.astype(BF16)


def _proj_diff_kernel(u_ref, w_ref, g_ref, o_ref, *, scale):
    y = jnp.dot(u_ref[...], w_ref[...], preferred_element_type=F32)
    tm = y.shape[0]
    lane = lax.broadcasted_iota(jnp.int32, (tm, LANES), 1)
    lo = lane < DIFF_QK
    gain = g_ref[...] * scale
    for h in range(N_HEADS):
        yh = y[:, h * HEAD_DIM:(h + 1) * HEAD_DIM]
        sq = yh * yh
        s_lo = jnp.sum(jnp.where(lo, sq, 0.0), axis=-1, keepdims=True)
        s_all = jnp.sum(sq, axis=-1, keepdims=True)
        ms = jnp.where(lo, s_lo, s_all - s_lo) * (1.0 / DIFF_QK)
        o_ref[:, h * HEAD_DIM:(h + 1) * HEAD_DIM] = (yh * lax.rsqrt(ms + EPS) * gain).astype(BF16)


def _proj(kern, u, w_sec, extra, extra_specs, out_width, out_dtype, tm, name):
    m, d = u.shape
    n = w_sec.shape[1]
    return pl.pallas_call(
        kern,
        grid=(m // tm,),
        in_specs=[
            pl.BlockSpec((tm, d), lambda i: (i, 0)),
            pl.BlockSpec((d, n), lambda i: (0, 0)),
        ] + extra_specs,
        out_specs=pl.BlockSpec((tm, out_width), lambda i: (i, 0)),
        out_shape=jax.ShapeDtypeStruct((m, out_width), out_dtype),
        compiler_params=_cparams(("arbitrary",)),
        name=name,
    )(u, w_sec, *extra)


def _softmax_step(s, v, m_ref, l_ref, acc_ref):
    m = m_ref[...]
    m_new = jnp.maximum(m, jnp.max(s, axis=-1, keepdims=True))
    p = jnp.exp2(s - m_new)
    alpha = jnp.exp2(m - m_new)
    m_ref[...] = m_new
    l_ref[...] = alpha * l_ref[...] + jnp.sum(p, axis=-1, keepdims=True)
    acc_ref[...] = alpha * acc_ref[...] + jnp.dot(p.astype(BF16), v, preferred_element_type=F32)


def _softmax_first(s, v, m_ref, l_ref, acc_ref):
    m = jnp.max(s, axis=-1, keepdims=True)
    p = jnp.exp2(s - m)
    m_ref[...] = m
    l_ref[...] = jnp.sum(p, axis=-1, keepdims=True)
    acc_ref[...] = jnp.dot(p.astype(BF16), v, preferred_element_type=F32)


_NT = (((1,), (1,)), ((), ()))


def _attn_scratch(rows):
    return [pltpu.VMEM((rows, 1), F32), pltpu.VMEM((rows, 1), F32), pltpu.VMEM((rows, HEAD_DIM), F32)]


def _fox_attn_kernel(q_ref, k_ref, v_ref, gate_ref, o_ref, m_ref, l_ref, acc_ref, *, blk):
    qi = pl.program_id(2)
    q = q_ref[0]
    stats = (m_ref, l_ref, acc_ref)

    def kv(j):
        start = pl.multiple_of(j * blk, blk)
        return k_ref[0, pl.ds(start, blk), :], v_ref[0, pl.ds(start, blk), :]

    k, v = kv(qi)
    s = lax.dot_general(q, k, _NT, preferred_element_type=F32)
    row = lax.broadcasted_iota(jnp.int32, (blk, blk), 0)
    col = lax.broadcasted_iota(jnp.int32, (blk, blk), 1)
    _softmax_first(jnp.where(col <= row, s, NEG), v, *stats)

    @pl.loop(0, qi)
    def _(j):
        k, v = kv(j)
        _softmax_step(lax.dot_general(q, k, _NT, preferred_element_type=F32), v, *stats)

    o_ref[0] = (acc_ref[...] / l_ref[...] * gate_ref[0]).astype(o_ref.dtype)


def _fox_attn(q, k, v, gate, blk):
    b, s, _ = v.shape
    return pl.pallas_call(
        functools.partial(_fox_attn_kernel, blk=blk),
        grid=(b, N_HEADS, s // blk),
        in_specs=[
            pl.BlockSpec((1, blk, AUG), lambda bi, h, qi: (bi, qi, h)),
            pl.BlockSpec((1, s, AUG), lambda bi, h, qi: (bi, 0, h)),
            pl.BlockSpec((1, s, HEAD_DIM), lambda bi, h, qi: (bi, 0, h)),
            pl.BlockSpec((1, blk, HEAD_DIM), lambda bi, h, qi: (bi, qi, h)),
        ],
        out_specs=pl.BlockSpec((1, blk, HEAD_DIM), lambda bi, h, qi: (bi, qi, h)),
        out_shape=jax.ShapeDtypeStruct((b, s, GROUP_WIDTH), BF16),
        scratch_shapes=_attn_scratch(blk),
        compiler_params=_cparams(("arbitrary", "arbitrary", "arbitrary")),
        name="fox_attn",
    )(q, k, v, gate)


def _diff_attn_kernel(q_ref, k_ref, v_ref, gate_ref, bias_ref, lam_ref, onorm_ref, o_ref,
                      m_ref, l_ref, acc_ref, *, blk, lambda_init):
    qi = pl.program_id(2)
    stats = (m_ref, l_ref, acc_ref)
    q = q_ref[0]
    lane = lax.broadcasted_iota(jnp.int32, q.shape, 1)
    zero = jnp.zeros_like(q)
    qz = jnp.concatenate([jnp.where(lane < DIFF_QK, q, zero), jnp.where(lane >= DIFF_QK, q, zero)], axis=0)

    def scores(j):
        start = pl.multiple_of(j * blk, blk)
        k = k_ref[0, pl.ds(start, blk), :]
        v = v_ref[0, pl.ds(start, blk), :]
        return lax.dot_general(qz, k, _NT, preferred_element_type=F32), v

    def biased(s, which):
        bt = bias_ref[0, which]
        return s + jnp.concatenate([bt, bt], axis=0)

    s, v = scores(qi)
    _softmax_first(biased(s, 0), v, *stats)
    n_far = jnp.maximum(qi - 1, 0)

    @pl.loop(0, n_far)
    def _(j):
        s, v = scores(j)
        _softmax_step(s, v, *stats)

    @pl.loop(n_far, qi)
    def _(j):
        s, v = scores(j)
        _softmax_step(biased(s, 1), v, *stats)

    lam_p = lam_ref[...]
    lam = (jnp.exp(jnp.sum(lam_p[0:1] * lam_p[1:2], axis=-1, keepdims=True))
           - jnp.exp(jnp.sum(lam_p[2:3] * lam_p[3:4], axis=-1, keepdims=True)) + lambda_init)
    o = acc_ref[...] / l_ref[...]
    o = o[:blk] - lam * o[blk:]
    ms = jnp.mean(o * o, axis=-1, keepdims=True)
    o = o * lax.rsqrt(ms + EPS) * onorm_ref[...] * (1.0 - lambda_init)
    o_ref[0] = (o * gate_ref[0]).astype(o_ref.dtype)


def _diff_attn(q, k, v, gate, bias_tiles, lam_params, onorm, blk, lambda_init):
    b, s, _ = v.shape
    return pl.pallas_call(
        functools.partial(_diff_attn_kernel, blk=blk, lambda_init=lambda_init),
        grid=(b, N_HEADS, s // blk),
        in_specs=[
            pl.BlockSpec((1, blk, HEAD_DIM), lambda bi, h, qi: (bi, qi, h)),
            pl.BlockSpec((1, s, HEAD_DIM), lambda bi, h, qi: (bi, 0, h)),
            pl.BlockSpec((1, s, HEAD_DIM), lambda bi, h, qi: (bi, 0, h)),
            pl.BlockSpec((1, blk, HEAD_DIM), lambda bi, h, qi: (bi, qi, h)),
            pl.BlockSpec((1, 2, blk, blk), lambda bi, h, qi: (h, 0, 0, 0)),
            pl.BlockSpec((4, DIFF_QK), lambda bi, h, qi: (0, 0)),
            pl.BlockSpec((1, HEAD_DIM), lambda bi, h, qi: (0, 0)),
        ],
        out_specs=pl.BlockSpec((1, blk, HEAD_DIM), lambda bi, h, qi: (bi, qi, h)),
        out_shape=jax.ShapeDtypeStruct((b, s, GROUP_WIDTH), BF16),
        scratch_shapes=_attn_scratch(2 * blk),
        compiler_params=_cparams(("arbitrary", "arbitrary", "arbitrary")),
        name="diff_attn",
    )(q, k, v, gate, bias_tiles, lam_params, onorm)


def _out_ple_kernel(x_ref, ma_ref, mb_ref, p_ref, wo_ref, wp_ref, gn_ref, wg_ref, o_ref):
    half = ma_ref.shape[1]
    h = x_ref[...]
    h = h + jnp.dot(ma_ref[...], wo_ref[:half, :], preferred_element_type=F32)
    h = h + jnp.dot(mb_ref[...], wo_ref[half:, :], preferred_element_type=F32)
    ms = jnp.mean(h * h, axis=-1, keepdims=True)
    hn = (h * lax.rsqrt(ms + EPS) * gn_ref[...]).astype(BF16)
    zg = jnp.dot(hn, wg_ref[...], preferred_element_type=F32)
    g = 1.0 / (1.0 + jnp.exp(-zg))
    e = jnp.dot(p_ref[...].astype(BF16), wp_ref[...], preferred_element_type=F32)
    o_ref[...] = h + g * e


def _out_ple(x2, mix_a, mix_b, p2, w_out, ple_proj, gate_norm, ple_gate, tm):
    m, d = x2.shape
    half = mix_a.shape[1]
    pdim = p2.shape[1]
    const = lambda i: (0, 0)
    single = dict(pipeline_mode=pl.Buffered(1))
    return pl.pallas_call(
        _out_ple_kernel,
        grid=(m // tm,),
        in_specs=[
            pl.BlockSpec((tm, d), lambda i: (i, 0)),
            pl.BlockSpec((tm, half), lambda i: (i, 0)),
            pl.BlockSpec((tm, half), lambda i: (i, 0)),
            pl.BlockSpec((tm, pdim), lambda i: (i, 0)),
            pl.BlockSpec((2 * half, d), const, **single),
            pl.BlockSpec((pdim, d), const, **single),
            pl.BlockSpec((1, d), const),
            pl.BlockSpec((d, d), const, **single),
        ],
        out_specs=pl.BlockSpec((tm, d), lambda i: (i, 0)),
        out_shape=jax.ShapeDtypeStruct((m, d), F32),
        compiler_params=_cparams(("arbitrary",)),
        name="out_ple",
    )(x2, mix_a, mix_b, p2, w_out, ple_proj, gate_norm, ple_gate)


def _t5_bucket(n):
    max_exact = N_BUCKETS // 2
    nf = jnp.maximum(n, 1).astype(F32)
    large = max_exact + (jnp.log(nf / max_exact) / math.log(MAX_DISTANCE / max_exact)
                         * (N_BUCKETS - max_exact)).astype(jnp.int32)
    large = jnp.minimum(large, N_BUCKETS - 1)
    return jnp.where(n < max_exact, n, large)


def _bias_tiles(rel_bias, blk):
    assert blk >= MAX_DISTANCE
    table = rel_bias.astype(F32)[_t5_bucket(jnp.arange(2 * blk))].T
    table = (table - rel_bias.astype(F32)[N_BUCKETS - 1][:, None]) * LOG2E
    r = jnp.arange(blk)[:, None]
    c = jnp.arange(blk)[None, :]
    diag = jnp.where(r >= c, table[:, jnp.clip(r - c, 0, None)], NEG)
    below = table[:, blk + r - c]
    return jnp.stack([diag, below], axis=1)


def kernel(x, p, attn_norm, w_in, b_forget, fox_q_norm, fox_k_norm, diff_q_norm, diff_k_norm,
           lambda_q1, lambda_k1, lambda_q2, lambda_k2, diff_out_norm, w_out, rel_bias,
           ple_proj, ple_gate_norm, ple_gate):
    b, s, d = x.shape
    depth = w_in.shape[0]
    gw = GROUP_WIDTH
    blk = 512
    tm = 1024
    bias_tiles = _bias_tiles(rel_bias, blk)

    h = x.reshape(b * s, d)
    for i in range(depth):
        lambda_init = 0.8 - 0.6 * math.exp(-0.3 * i)
        w = w_in[i].astype(BF16)
        w_f = jnp.pad(w[:, 8 * gw:], ((0, 0), (0, LANES - N_HEADS)))
        b_f = jnp.pad(b_forget[i].astype(F32), (0, LANES - N_HEADS)).reshape(1, LANES)
        u, c2 = _norm_forget(h, attn_norm[i].astype(F32).reshape(1, d), w_f, b_f, s, 512)

        sec = lambda j: w[:, j * gw:(j + 1) * gw]
        row = lambda a: a.astype(F32).reshape(1, -1)
        vec_spec = lambda n: pl.BlockSpec((1, n), lambda r: (0, 0))
        c_spec = pl.BlockSpec((tm, LANES), lambda r: (r, 0))
        fox_scale = HEAD_DIM ** -0.5 * LOG2E
        diff_scale = DIFF_QK ** -0.5 * LOG2E

        qa = _proj(functools.partial(_proj_fox_kernel, is_q=True, scale=fox_scale), u, sec(0),
                   [row(fox_q_norm[i]), c2], [vec_spec(HEAD_DIM), c_spec], N_HEADS * AUG, BF16, tm, "proj_fox_q")
        ka = _proj(functools.partial(_proj_fox_kernel, is_q=False, scale=1.0), u, sec(1),
                   [row(fox_k_norm[i]), c2], [vec_spec(HEAD_DIM), c_spec], N_HEADS * AUG, BF16, tm, "proj_fox_k")
        va = _proj(_proj_plain_kernel, u, sec(2), [], [], gw, BF16, tm, "proj_fox_v")
        ga = _proj(_proj_silu_kernel, u, sec(3), [], [], gw, F32, tm, "proj_fox_gate")
        dq_gain = row(jnp.tile(diff_q_norm[i], 2))
        dk_gain = row(jnp.tile(diff_k_norm[i], 2))
        qb = _proj(functools.partial(_proj_diff_kernel, scale=diff_scale), u, sec(4),
                   [dq_gain], [vec_spec(HEAD_DIM)], gw, BF16, tm, "proj_diff_q")
        kb = _proj(functools.partial(_proj_diff_kernel, scale=1.0), u, sec(5),
                   [dk_gain], [vec_spec(HEAD_DIM)], gw, BF16, tm, "proj_diff_k")
        vb = _proj(_proj_plain_kernel, u, sec(6), [], [], gw, BF16, tm, "proj_diff_v")
        gb = _proj(_proj_silu_kernel, u, sec(7), [], [], gw, F32, tm, "proj_diff_gate")

        to3 = lambda a: a.reshape(b, s, a.shape[-1])
        mix_a = _fox_attn(to3(qa), to3(ka), to3(va), to3(ga), blk)
        lam_params = jnp.stack([lambda_q1[i], lambda_k1[i], lambda_q2[i], lambda_k2[i]]).astype(F32)
        mix_b = _diff_attn(to3(qb), to3(kb), to3(vb), to3(gb), bias_tiles, lam_params,
                           row(diff_out_norm[i]), blk, lambda_init)

        h = _out_ple(h, mix_a.reshape(b * s, gw), mix_b.reshape(b * s, gw),
                     p[i].reshape(b * s, -1), w_out[i].astype(BF16), ple_proj[i].astype(BF16),
                     row(ple_gate_norm[i]), ple_gate[i].astype(BF16), 256)
    return h.reshape(b, s, d)
```

```python
import functools
import math

import jax
import jax.numpy as jnp
from jax import lax
from jax.experimental import pallas as pl
from jax.experimental.pallas import tpu as pltpu

F32 = jnp.float32
BF16 = jnp.bfloat16

HEAD_DIM = 128
N_HEADS = 8
DIFF_QK = HEAD_DIM // 2
GROUP_WIDTH = N_HEADS * HEAD_DIM
N_BUCKETS = 32
MAX_DISTANCE = 128
EPS = 1e-6
LOG2E = math.log2(math.e)
NEG = -1e30
LANES = 128
AUG = 2 * HEAD_DIM
VMEM_LIMIT = 56 * 1024 * 1024


def _cparams(sem):
    return pltpu.CompilerParams(dimension_semantics=sem, vmem_limit_bytes=VMEM_LIMIT)


def _split3(v):
    p0 = v.astype(BF16).astype(F32)
    r1 = v - p0
    p1 = r1.astype(BF16).astype(F32)
    p2 = (r1 - p1).astype(BF16).astype(F32)
    return p0, p1, p2


def _norm_forget_kernel(x_ref, g_ref, wf_ref, bf_ref, u_ref, c_ref, carry_ref, *, tiles_per_seq):
    i = pl.program_id(0)
    x = x_ref[...]
    tm = x.shape[0]
    ms = jnp.mean(x * x, axis=-1, keepdims=True)
    u = (x * lax.rsqrt(ms + EPS) * g_ref[...]).astype(BF16)
    u_ref[...] = u

    z = jnp.dot(u, wf_ref[...], preferred_element_type=F32) + bf_ref[...]
    lf = (jnp.minimum(z, 0.0) - jnp.log(1.0 + jnp.exp(-jnp.abs(z)))) * LOG2E

    @pl.when(i % tiles_per_seq == 0)
    def _():
        carry_ref[...] = jnp.zeros_like(carry_ref)

    row = lax.broadcasted_iota(jnp.int32, (tm, tm), 0)
    col = lax.broadcasted_iota(jnp.int32, (tm, tm), 1)
    tri = jnp.where(col <= row, 1.0, 0.0).astype(BF16)
    c = carry_ref[...]
    for part in _split3(lf):
        c = c + jnp.dot(tri, part.astype(BF16), preferred_element_type=F32)
    c_ref[...] = c
    carry_ref[...] = c[tm - 1:tm, :]


def _norm_forget(x2, gain, w_f, b_f, seq, tm):
    m, d = x2.shape
    kern = functools.partial(_norm_forget_kernel, tiles_per_seq=seq // tm)
    return pl.pallas_call(
        kern,
        grid=(m // tm,),
        in_specs=[
            pl.BlockSpec((tm, d), lambda i: (i, 0)),
            pl.BlockSpec((1, d), lambda i: (0, 0)),
            pl.BlockSpec((d, LANES), lambda i: (0, 0)),
            pl.BlockSpec((1, LANES), lambda i: (0, 0)),
        ],
        out_specs=[
            pl.BlockSpec((tm, d), lambda i: (i, 0)),
            pl.BlockSpec((tm, LANES), lambda i: (i, 0)),
        ],
        out_shape=[
            jax.ShapeDtypeStruct((m, d), BF16),
            jax.ShapeDtypeStruct((m, LANES), F32),
        ],
        scratch_shapes=[pltpu.VMEM((1, LANES), F32)],
        compiler_params=_cparams(("arbitrary",)),
        name="norm_forget",
    )(x2, gain, w_f, b_f)


def _proj_plain_kernel(u_ref, w_ref, o_ref):
    y = jnp.dot(u_ref[...], w_ref[...], preferred_element_type=F32)
    o_ref[...] = y.astype(o_ref.dtype)


def _proj_silu_kernel(u_ref, w_ref, o_ref):
    y = jnp.dot(u_ref[...], w_ref[...], preferred_element_type=F32)
    o_ref[...] = (y * (1.0 / (1.0 + jnp.exp(-y)))).astype(o_ref.dtype)


def _proj_fox_kernel(u_ref, w_ref, g_ref, c_ref, o_ref, *, is_q, scale):
    y = jnp.dot(u_ref[...], w_ref[...], preferred_element_type=F32)
    tm = y.shape[0]
    lane = lax.broadcasted_iota(jnp.int32, (tm, LANES), 1)
    gain = g_ref[...] * scale
    c_all = c_ref[...]
    for h in range(N_HEADS):
        yh = y[:, h * HEAD_DIM:(h + 1) * HEAD_DIM]
        ms = jnp.mean(yh * yh, axis=-1, keepdims=True)
        o_ref[:, h * AUG:h * AUG + HEAD_DIM] = (yh * lax.rsqrt(ms + EPS) * gain).astype(BF16)
        ch = jnp.broadcast_to(c_all[:, h:h + 1], (tm, LANES))
        if not is_q:
            ch = -ch
        p0, p1, p2 = _split3(ch)
        first = 0 if is_q else 3
        ones_at = 3 if is_q else 0
        aug = jnp.where(lane == first, p0, jnp.where(lane == first + 1, p1, jnp.where(lane == first + 2, p2, 0.0)))
        aug = jnp.where((lane >= ones_at) & (lane < ones_at + 3), 1.0, aug)
        o_ref[:, h * AUG + HEAD_DIM:(h + 1) * AUG] = aug.astype(BF16)


def _proj_diff_kernel(u_ref, w_ref, g_ref, o_ref, *, scale):
    y = jnp.dot(u_ref[...], w_ref[...], preferred_element_type=F32)
    tm = y.shape[0]
    lane = lax.broadcasted_iota(jnp.int32, (tm, LANES), 1)
    lo = lane < DIFF_QK
    gain = g_ref[...] * scale
    for h in range(N_HEADS):
        yh = y[:, h * HEAD_DIM:(h + 1) * HEAD_DIM]
        sq = yh * yh
        s_lo = jnp.sum(jnp.where(lo, sq, 0.0), axis=-1, keepdims=True)
        s_all = jnp.sum(sq, axis=-1, keepdims=True)
        ms = jnp.where(lo, s_lo, s_all - s_lo) * (1.0 / DIFF_QK)
        o_ref[:, h * HEAD_DIM:(h + 1) * HEAD_DIM] = (yh * lax.rsqrt(ms + EPS) * gain).astype(BF16)


def _proj(kern, u, w_sec, extra, extra_specs, out_width, out_dtype, tm, name):
    m, d = u.shape
    n = w_sec.shape[1]
    return pl.pallas_call(
        kern,
        grid=(m // tm,),
        in_specs=[
            pl.BlockSpec((tm, d), lambda i: (i, 0)),
            pl.BlockSpec((d, n), lambda i: (0, 0)),
        ] + extra_specs,
        out_specs=pl.BlockSpec((tm, out_width), lambda i: (i, 0)),
        out_shape=jax.ShapeDtypeStruct((m, out_width), out_dtype),
        compiler_params=_cparams(("arbitrary",)),
        name=name,
    )(u, w_sec, *extra)


V_ROWS = HEAD_DIM + 16


def _fill_vt(v_ref, vt_ref, blk):
    tail_row = lax.broadcasted_iota(jnp.int32, (V_ROWS - HEAD_DIM, blk), 0)
    tail = jnp.where(tail_row == 0, 1.0, 0.0).astype(BF16)

    @pl.loop(0, vt_ref.shape[0])
    def _(j):
        start = pl.multiple_of(j * blk, blk)
        vt_ref[j, :HEAD_DIM, :] = v_ref[0, pl.ds(start, blk), :].astype(F32).T.astype(BF16)
        vt_ref[j, HEAD_DIM:, :] = tail


def _attn_scratch(n_blocks, blk, width):
    return [
        pltpu.VMEM((n_blocks, V_ROWS, blk), BF16),
        pltpu.VMEM((2, blk, width), F32),
        pltpu.VMEM((2, 1, width), F32),
        pltpu.VMEM((2, blk, width), BF16),
        pltpu.VMEM((2, 1, width), F32),
        pltpu.VMEM((1, width), F32),
        pltpu.VMEM((V_ROWS, width), F32),
    ]


def _attn_pipeline(qi, scores, vt_ref, s_ref, mx_ref, p_ref, al_ref, m_ref, acc_ref):
    n = qi + 1

    def key_block(t):
        return jnp.maximum(qi - t, 0)

    def qk(t, kind, slot):
        st = scores(key_block(t), kind)
        s_ref[slot] = st
        mx_ref[slot] = jnp.max(st, axis=0, keepdims=True)

    def softmax(slot):
        m_old = m_ref[...]
        m_new = jnp.maximum(m_old, mx_ref[slot])
        m_ref[...] = m_new
        al_ref[slot] = jnp.exp2(m_old - m_new)
        p_ref[slot] = jnp.exp2(s_ref[slot] - m_new).astype(BF16)

    def pv(t, slot):
        acc_ref[...] = al_ref[slot] * acc_ref[...] + jnp.dot(
            vt_ref[key_block(t)], p_ref[slot], preferred_element_type=F32)

    m_ref[...] = jnp.full(m_ref.shape, NEG, F32)
    acc_ref[...] = jnp.zeros(acc_ref.shape, F32)
    qk(0, "diag", 0)
    softmax(0)
    qk(1, "near", 1)

    def body(t, slot):
        pv(t - 1, 1 - slot)
        softmax(slot)
        qk(t + 1, "far", 1 - slot)

    @pl.loop(0, (n - 1) // 2)
    def _(u):
        body(2 * u + 1, 1)
        body(2 * u + 2, 0)

    @pl.when((n - 1) % 2 == 1)
    def _():
        body(n - 1, 1)

    last = (n - 1) % 2
    acc_ref[...] = al_ref[last] * acc_ref[...] + jnp.dot(
        vt_ref[key_block(n - 1)], p_ref[last], preferred_element_type=F32)


def _fox_attn_kernel(q_ref, k_ref, v_ref, gate_ref, o_ref, vt_ref, *scratch, blk):
    qi = pl.program_id(2)

    @pl.when(qi == 0)
    def _():
        _fill_vt(v_ref, vt_ref, blk)

    qt = q_ref[0].astype(F32).T.astype(BF16)
    key = lax.broadcasted_iota(jnp.int32, (blk, blk), 0)
    qry = lax.broadcasted_iota(jnp.int32, (blk, blk), 1)

    def scores(kb, kind):
        start = pl.multiple_of(kb * blk, blk)
        st = jnp.dot(k_ref[0, pl.ds(start, blk), :], qt, preferred_element_type=F32)
        return jnp.where(key <= qry, st, NEG) if kind == "diag" else st

    _attn_pipeline(qi, scores, vt_ref, *scratch)
    acc = scratch[-1][...]
    ot = acc[:HEAD_DIM] / acc[HEAD_DIM:HEAD_DIM + 1]
    o_ref[0] = (ot.T * gate_ref[0]).astype(o_ref.dtype)


def _fox_attn(q, k, v, gate, blk):
    b, s, _ = v.shape
    return pl.pallas_call(
        functools.partial(_fox_attn_kernel, blk=blk),
        grid=(b, N_HEADS, s // blk),
        in_specs=[
            pl.BlockSpec((1, blk, AUG), lambda bi, h, qi: (bi, qi, h)),
            pl.BlockSpec((1, s, AUG), lambda bi, h, qi: (bi, 0, h)),
            pl.BlockSpec((1, s, HEAD_DIM), lambda bi, h, qi: (bi, 0, h)),
            pl.BlockSpec((1, blk, HEAD_DIM), lambda bi, h, qi: (bi, qi, h)),
        ],
        out_specs=pl.BlockSpec((1, blk, HEAD_DIM), lambda bi, h, qi: (bi, qi, h)),
        out_shape=jax.ShapeDtypeStruct((b, s, GROUP_WIDTH), BF16),
        scratch_shapes=_attn_scratch(s // blk, blk, blk),
        compiler_params=_cparams(("arbitrary", "arbitrary", "arbitrary")),
        name="fox_attn",
    )(q, k, v, gate)


def _diff_attn_kernel(q_ref, k_ref, v_ref, gate_ref, table_ref, lam_ref, onorm_ref, o_ref, bias_ref, vt_ref,
                      *scratch, blk, lambda_init):
    qi = pl.program_id(2)

    @pl.when(qi == 0)
    def _():
        _fill_vt(v_ref, vt_ref, blk)
        shifted = pltpu.roll(jnp.broadcast_to(table_ref[0], (blk, 2 * blk)), 0, 1, stride=1, stride_axis=0)
        key = lax.broadcasted_iota(jnp.int32, (blk, blk), 0)
        qry = lax.broadcasted_iota(jnp.int32, (blk, blk), 1)
        bias_ref[0] = jnp.where(key <= qry, shifted[:, :blk], NEG)
        bias_ref[1] = shifted[:, blk:]

    qt = q_ref[0].astype(F32).T
    dim = lax.broadcasted_iota(jnp.int32, qt.shape, 0)
    qzt = jnp.concatenate([jnp.where(dim < DIFF_QK, qt, 0.0), jnp.where(dim >= DIFF_QK, qt, 0.0)],
                          axis=1).astype(BF16)

    def scores(kb, kind):
        start = pl.multiple_of(kb * blk, blk)
        st = jnp.dot(k_ref[0, pl.ds(start, blk), :], qzt, preferred_element_type=F32)
        if kind == "far":
            return st
        bt = bias_ref[0 if kind == "diag" else 1]
        return st + jnp.concatenate([bt, bt], axis=1)

    _attn_pipeline(qi, scores, vt_ref, *scratch)

    lam_p = lam_ref[...]
    lam = (jnp.exp(jnp.sum(lam_p[0:1] * lam_p[1:2], axis=-1, keepdims=True))
           - jnp.exp(jnp.sum(lam_p[2:3] * lam_p[3:4], axis=-1, keepdims=True)) + lambda_init)
    acc = scratch[-1][...]
    ot = acc[:HEAD_DIM] / acc[HEAD_DIM:HEAD_DIM + 1]
    o = (ot[:, :blk] - lam * ot[:, blk:]).T
    ms = jnp.mean(o * o, axis=-1, keepdims=True)
    o = o * lax.rsqrt(ms + EPS) * onorm_ref[...] * (1.0 - lambda_init)
    o_ref[0] = (o * gate_ref[0]).astype(o_ref.dtype)


def _diff_attn(q, k, v, gate, bias_table, lam_params, onorm, blk, lambda_init):
    b, s, _ = v.shape
    return pl.pallas_call(
        functools.partial(_diff_attn_kernel, blk=blk, lambda_init=lambda_init),
        grid=(b, N_HEADS, s // blk),
        in_specs=[
            pl.BlockSpec((1, blk, HEAD_DIM), lambda bi, h, qi: (bi, qi, h)),
            pl.BlockSpec((1, s, HEAD_DIM), lambda bi, h, qi: (bi, 0, h)),
            pl.BlockSpec((1, s, HEAD_DIM), lambda bi, h, qi: (bi, 0, h)),
            pl.BlockSpec((1, blk, HEAD_DIM), lambda bi, h, qi: (bi, qi, h)),
            pl.BlockSpec((1, 1, 2 * blk), lambda bi, h, qi: (h, 0, 0)),
            pl.BlockSpec((4, DIFF_QK), lambda bi, h, qi: (0, 0)),
            pl.BlockSpec((1, HEAD_DIM), lambda bi, h, qi: (0, 0)),
        ],
        out_specs=pl.BlockSpec((1, blk, HEAD_DIM), lambda bi, h, qi: (bi, qi, h)),
        out_shape=jax.ShapeDtypeStruct((b, s, GROUP_WIDTH), BF16),
        scratch_shapes=[pltpu.VMEM((2, blk, blk), F32)] + _attn_scratch(s // blk, blk, 2 * blk),
        compiler_params=_cparams(("arbitrary", "arbitrary", "arbitrary")),
        name="diff_attn",
    )(q, k, v, gate, bias_table, lam_params, onorm)


def _out_ple_kernel(x_ref, ma_ref, mb_ref, p_ref, wo_ref, wp_ref, gn_ref, wg_ref, o_ref):
    half = ma_ref.shape[1]
    h = x_ref[...]
    h = h + jnp.dot(ma_ref[...], wo_ref[:half, :], preferred_element_type=F32)
    h = h + jnp.dot(mb_ref[...], wo_ref[half:, :], preferred_element_type=F32)
    ms = jnp.mean(h * h, axis=-1, keepdims=True)
    hn = (h * lax.rsqrt(ms + EPS) * gn_ref[...]).astype(BF16)
    zg = jnp.dot(hn, wg_ref[...], preferred_element_type=F32)
    g = 1.0 / (1.0 + jnp.exp(-zg))
    e = jnp.dot(p_ref[...].astype(BF16), wp_ref[...], preferred_element_type=F32)
    o_ref[...] = h + g * e


def _out_ple(x2, mix_a, mix_b, p2, w_out, ple_proj, gate_norm, ple_gate, tm):
    m, d = x2.shape
    half = mix_a.shape[1]
    pdim = p2.shape[1]
    const = lambda i: (0, 0)
    single = dict(pipeline_mode=pl.Buffered(1))
    return pl.pallas_call(
        _out_ple_kernel,
        grid=(m // tm,),
        in_specs=[
            pl.BlockSpec((tm, d), lambda i: (i, 0)),
            pl.BlockSpec((tm, half), lambda i: (i, 0)),
            pl.BlockSpec((tm, half), lambda i: (i, 0)),
            pl.BlockSpec((tm, pdim), lambda i: (i, 0)),
            pl.BlockSpec((2 * half, d), const, **single),
            pl.BlockSpec((pdim, d), const, **single),
            pl.BlockSpec((1, d), const),
            pl.BlockSpec((d, d), const, **single),
        ],
        out_specs=pl.BlockSpec((tm, d), lambda i: (i, 0)),
        out_shape=jax.ShapeDtypeStruct((m, d), F32),
        compiler_params=_cparams(("arbitrary",)),
        name="out_ple",
    )(x2, mix_a, mix_b, p2, w_out, ple_proj, gate_norm, ple_gate)


def _t5_bucket(n):
    max_exact = N_BUCKETS // 2
    nf = jnp.maximum(n, 1).astype(F32)
    large = max_exact + (jnp.log(nf / max_exact) / math.log(MAX_DISTANCE / max_exact)
                         * (N_BUCKETS - max_exact)).astype(jnp.int32)
    large = jnp.minimum(large, N_BUCKETS - 1)
    return jnp.where(n < max_exact, n, large)


def _bias_table(rel_bias, blk):
    assert blk >= MAX_DISTANCE
    table = rel_bias.astype(F32)[_t5_bucket(jnp.arange(2 * blk))].T
    table = (table - rel_bias.astype(F32)[N_BUCKETS - 1][:, None]) * LOG2E
    return table[:, None, :]


def kernel(x, p, attn_norm, w_in, b_forget, fox_q_norm, fox_k_norm, diff_q_norm, diff_k_norm,
           lambda_q1, lambda_k1, lambda_q2, lambda_k2, diff_out_norm, w_out, rel_bias,
           ple_proj, ple_gate_norm, ple_gate):
    b, s, d = x.shape
    depth = w_in.shape[0]
    gw = GROUP_WIDTH
    blk = 512
    tm = 1024
    bias_table = _bias_table(rel_bias, blk)

    h = x.reshape(b * s, d)
    for i in range(depth):
        lambda_init = 0.8 - 0.6 * math.exp(-0.3 * i)
        w = w_in[i].astype(BF16)
        w_f = jnp.pad(w[:, 8 * gw:], ((0, 0), (0, LANES - N_HEADS)))
        b_f = jnp.pad(b_forget[i].astype(F32), (0, LANES - N_HEADS)).reshape(1, LANES)
        u, c2 = _norm_forget(h, attn_norm[i].astype(F32).reshape(1, d), w_f, b_f, s, 512)

        sec = lambda j: w[:, j * gw:(j + 1) * gw]
        row = lambda a: a.astype(F32).reshape(1, -1)
        vec_spec = lambda n: pl.BlockSpec((1, n), lambda r: (0, 0))
        c_spec = pl.BlockSpec((tm, LANES), lambda r: (r, 0))
        fox_scale = HEAD_DIM ** -0.5 * LOG2E
        diff_scale = DIFF_QK ** -0.5 * LOG2E

        qa = _proj(functools.partial(_proj_fox_kernel, is_q=True, scale=fox_scale), u, sec(0),
                   [row(fox_q_norm[i]), c2], [vec_spec(HEAD_DIM), c_spec], N_HEADS * AUG, BF16, tm, "proj_fox_q")
        ka = _proj(functools.partial(_proj_fox_kernel, is_q=False, scale=1.0), u, sec(1),
                   [row(fox_k_norm[i]), c2], [vec_spec(HEAD_DIM), c_spec], N_HEADS * AUG, BF16, tm, "proj_fox_k")
        va = _proj(_proj_plain_kernel, u, sec(2), [], [], gw, BF16, tm, "proj_fox_v")
        ga = _proj(_proj_silu_kernel, u, sec(3), [], [], gw, F32, tm, "proj_fox_gate")
        dq_gain = row(jnp.tile(diff_q_norm[i], 2))
        dk_gain = row(jnp.tile(diff_k_norm[i], 2))
        qb = _proj(functools.partial(_proj_diff_kernel, scale=diff_scale), u, sec(4),
                   [dq_gain], [vec_spec(HEAD_DIM)], gw, BF16, tm, "proj_diff_q")
        kb = _proj(functools.partial(_proj_diff_kernel, scale=1.0), u, sec(5),
                   [dk_gain], [vec_spec(HEAD_DIM)], gw, BF16, tm, "proj_diff_k")
        vb = _proj(_proj_plain_kernel, u, sec(6), [], [], gw, BF16, tm, "proj_diff_v")
        gb = _proj(_proj_silu_kernel, u, sec(7), [], [], gw, F32, tm, "proj_diff_gate")

        to3 = lambda a: a.reshape(b, s, a.shape[-1])
        mix_a = _fox_attn(to3(qa), to3(ka), to3(va), to3(ga), blk)
        lam_params = jnp.stack([lambda_q1[i], lambda_k1[i], lambda_q2[i], lambda_k2[i]]).astype(F32)
        mix_b = _diff_attn(to3(qb), to3(kb), to3(vb), to3(gb), bias_table, lam_params,
                           row(diff_out_norm[i]), blk, lambda_init)

        h = _out_ple(h, mix_a.reshape(b * s, gw), mix_b.reshape(b * s, gw),
                     p[i].reshape(b * s, -1), w_out[i].astype(BF16), ple_proj[i].astype(BF16),
                     row(ple_gate_norm[i]), ple_gate[i].astype(BF16), 256)
    return h.reshape(b, s, d)
```

```python
import functools
import math

import jax
import jax.numpy as jnp
from jax import lax
from jax.experimental import pallas as pl
from jax.experimental.pallas import tpu as pltpu

F32 = jnp.float32
BF16 = jnp.bfloat16

HEAD_DIM = 128
N_HEADS = 8
DIFF_QK = HEAD_DIM // 2
GROUP_WIDTH = N_HEADS * HEAD_DIM
N_BUCKETS = 32
MAX_DISTANCE = 128
EPS = 1e-6
LOG2E = math.log2(math.e)
MASKED = -3e38
M_FLOOR = -(2.0 ** 100)
Q_PER_K = 2
QUERY_CHUNK = 256
LANES = 128
AUG = 2 * HEAD_DIM
VMEM_LIMIT = 56 * 1024 * 1024


def _cparams(sem):
    return pltpu.CompilerParams(dimension_semantics=sem, vmem_limit_bytes=VMEM_LIMIT)


def _split3(v):
    p0 = v.astype(BF16).astype(F32)
    r1 = v - p0
    p1 = r1.astype(BF16).astype(F32)
    p2 = (r1 - p1).astype(BF16).astype(F32)
    return p0, p1, p2


def _norm_forget_kernel(x_ref, g_ref, wf_ref, bf_ref, u_ref, c_ref, carry_ref, *, tiles_per_seq):
    i = pl.program_id(0)
    x = x_ref[...]
    tm = x.shape[0]
    ms = jnp.mean(x * x, axis=-1, keepdims=True)
    u = (x * lax.rsqrt(ms + EPS) * g_ref[...]).astype(BF16)
    u_ref[...] = u

    z = jnp.dot(u, wf_ref[...], preferred_element_type=F32) + bf_ref[...]
    lf = (jnp.minimum(z, 0.0) - jnp.log(1.0 + jnp.exp(-jnp.abs(z)))) * LOG2E

    @pl.when(i % tiles_per_seq == 0)
    def _():
        carry_ref[...] = jnp.zeros_like(carry_ref)

    row = lax.broadcasted_iota(jnp.int32, (tm, tm), 0)
    col = lax.broadcasted_iota(jnp.int32, (tm, tm), 1)
    tri = jnp.where(col <= row, 1.0, 0.0).astype(BF16)
    c = carry_ref[...]
    for part in _split3(lf):
        c = c + jnp.dot(tri, part.astype(BF16), preferred_element_type=F32)
    c_ref[...] = c
    carry_ref[...] = c[tm - 1:tm, :]


def _norm_forget(x2, gain, w_f, b_f, seq, tm):
    m, d = x2.shape
    kern = functools.partial(_norm_forget_kernel, tiles_per_seq=seq // tm)
    return pl.pallas_call(
        kern,
        grid=(m // tm,),
        in_specs=[
            pl.BlockSpec((tm, d), lambda i: (i, 0)),
            pl.BlockSpec((1, d), lambda i: (0, 0)),
            pl.BlockSpec((d, LANES), lambda i: (0, 0)),
            pl.BlockSpec((1, LANES), lambda i: (0, 0)),
        ],
        out_specs=[
            pl.BlockSpec((tm, d), lambda i: (i, 0)),
            pl.BlockSpec((tm, LANES), lambda i: (i, 0)),
        ],
        out_shape=[
            jax.ShapeDtypeStruct((m, d), BF16),
            jax.ShapeDtypeStruct((m, LANES), F32),
        ],
        scratch_shapes=[pltpu.VMEM((1, LANES), F32)],
        compiler_params=_cparams(("arbitrary",)),
        name="norm_forget",
    )(x2, gain, w_f, b_f)


def _proj_plain_kernel(u_ref, w_ref, o_ref):
    y = jnp.dot(u_ref[...], w_ref[...], preferred_element_type=F32)
    o_ref[...] = y.astype(o_ref.dtype)


def _proj_silu_kernel(u_ref, w_ref, o_ref):
    y = jnp.dot(u_ref[...], w_ref[...], preferred_element_type=F32)
    o_ref[...] = (y * (1.0 / (1.0 + jnp.exp(-y)))).astype(o_ref.dtype)


def _proj_fox_kernel(u_ref, w_ref, g_ref, c_ref, o_ref, *, is_q, scale):
    y = jnp.dot(u_ref[...], w_ref[...], preferred_element_type=F32)
    tm = y.shape[0]
    lane = lax.broadcasted_iota(jnp.int32, (tm, LANES), 1)
    gain = g_ref[...] * scale
    c_all = c_ref[...]
    for h in range(N_HEADS):
        yh = y[:, h * HEAD_DIM:(h + 1) * HEAD_DIM]
        ms = jnp.mean(yh * yh, axis=-1, keepdims=True)
        o_ref[:, h * AUG:h * AUG + HEAD_DIM] = (yh * lax.rsqrt(ms + EPS) * gain).astype(BF16)
        ch = jnp.broadcast_to(c_all[:, h:h + 1], (tm, LANES))
        if not is_q:
            ch = -ch
        p0, p1, p2 = _split3(ch)
        first = 0 if is_q else 3
        ones_at = 3 if is_q else 0
        aug = jnp.where(lane == first, p0, jnp.where(lane == first + 1, p1, jnp.where(lane == first + 2, p2, 0.0)))
        aug = jnp.where((lane >= ones_at) & (lane < ones_at + 3), 1.0, aug)
        o_ref[:, h * AUG + HEAD_DIM:(h + 1) * AUG] = aug.astype(BF16)


def _proj_diff_kernel(u_ref, w_ref, g_ref, o_ref, *, scale):
    y = jnp.dot(u_ref[...], w_ref[...], preferred_element_type=F32)
    tm = y.shape[0]
    lane = lax.broadcasted_iota(jnp.int32, (tm, LANES), 1)
    lo = lane < DIFF_QK
    gain = g_ref[...] * scale
    for h in range(N_HEADS):
        yh = y[:, h * HEAD_DIM:(h + 1) * HEAD_DIM]
        sq = yh * yh
        s_lo = jnp.sum(jnp.where(lo, sq, 0.0), axis=-1, keepdims=True)
        s_all = jnp.sum(sq, axis=-1, keepdims=True)
        ms = jnp.where(lo, s_lo, s_all - s_lo) * (1.0 / DIFF_QK)
        o_ref[:, h * HEAD_DIM:(h + 1) * HEAD_DIM] = (yh * lax.rsqrt(ms + EPS) * gain).astype(BF16)


def _proj(kern, u, w_sec, extra, extra_specs, out_width, out_dtype, tm, name):
    m, d = u.shape
    n = w_sec.shape[1]
    return pl.pallas_call(
        kern,
        grid=(m // tm,),
        in_specs=[
            pl.BlockSpec((tm, d), lambda i: (i, 0)),
            pl.BlockSpec((d, n), lambda i: (0, 0)),
        ] + extra_specs,
        out_specs=pl.BlockSpec((tm, out_width), lambda i: (i, 0)),
        out_shape=jax.ShapeDtypeStruct((m, out_width), out_dtype),
        compiler_params=_cparams(("arbitrary",)),
        name=name,
    )(u, w_sec, *extra)


V_ROWS = HEAD_DIM + 16


def _fill_vt(v_ref, vt_ref, blk):
    tail_row = lax.broadcasted_iota(jnp.int32, (V_ROWS - HEAD_DIM, blk), 0)
    tail = jnp.where(tail_row == 0, 1.0, 0.0).astype(BF16)

    @pl.loop(0, vt_ref.shape[0])
    def _(j):
        start = pl.multiple_of(j * blk, blk)
        vt_ref[j, :HEAD_DIM, :] = v_ref[0, pl.ds(start, blk), :].astype(F32).T.astype(BF16)
        vt_ref[j, HEAD_DIM:, :] = tail


def _attn_scratch(n_blocks, blk, width):
    return [
        pltpu.VMEM((n_blocks, V_ROWS, blk), BF16),
        pltpu.VMEM((2, blk, width), BF16),
        pltpu.VMEM((2, 1, width), F32),
        pltpu.VMEM((2, blk, width), BF16),
        pltpu.VMEM((2, 1, width), F32),
        pltpu.VMEM((1, width), F32),
        pltpu.VMEM((V_ROWS, width), F32),
    ]


def _attn_pipeline(qi, scores, vt_ref, s_ref, mx_ref, p_ref, al_ref, m_ref, acc_ref):
    n = Q_PER_K * (qi + 1)

    def key_block(t):
        return jnp.maximum(n - 1 - t, 0)

    width = s_ref.shape[-1]
    everything = slice(0, width)

    def qk(t, kind, slot, lanes=everything):
        st = scores(key_block(t), kind, lanes)
        s_ref[slot, :, lanes] = st.astype(BF16)
        mx_ref[slot, :, lanes] = jnp.max(st, axis=0, keepdims=True)

    def softmax(slot, lanes=everything):
        m_old = m_ref[:, lanes]
        m_new = jnp.maximum(m_old, mx_ref[slot, :, lanes]).astype(BF16)
        m_ref[:, lanes] = m_new.astype(F32)
        al_ref[slot, :, lanes] = jnp.exp2(m_old - m_new.astype(F32))
        p_ref[slot, :, lanes] = jnp.exp2(s_ref[slot, :, lanes] - m_new)

    def pv(t, slot, lanes=everything):
        acc_ref[:, lanes] = al_ref[slot, :, lanes] * acc_ref[:, lanes] + jnp.dot(
            vt_ref[key_block(t)], p_ref[slot, :, lanes], preferred_element_type=F32)

    m_ref[...] = jnp.full(m_ref.shape, M_FLOOR, F32)
    acc_ref[...] = jnp.zeros(acc_ref.shape, F32)
    qk(0, "diag0", 0)
    softmax(0)
    qk(1, "diag1", 1)

    def body(t, slot, kind="far"):
        if kind != "far":
            pv(t - 1, 1 - slot)
            softmax(slot)
            qk(t + 1, kind, 1 - slot)
            return
        for j in range(width // QUERY_CHUNK):
            lanes = slice(j * QUERY_CHUNK, (j + 1) * QUERY_CHUNK)
            softmax(slot, lanes)
            pv(t - 1, 1 - slot, lanes)
            qk(t + 1, kind, 1 - slot, lanes)

    body(1, 1, "near")

    @pl.loop(0, qi)
    def _(u):
        body(2 * u + 2, 0)
        body(2 * u + 3, 1)

    pv(n - 1, 1)


def _fox_attn_kernel(q_ref, k_ref, v_ref, gate_ref, o_ref, vt_ref, *scratch, blk):
    qi = pl.program_id(2)

    @pl.when(qi == 0)
    def _():
        _fill_vt(v_ref, vt_ref, blk)

    qt = q_ref[0].astype(F32).T.astype(BF16)
    key = lax.broadcasted_iota(jnp.int32, (blk, blk), 0)
    qry = lax.broadcasted_iota(jnp.int32, (blk, blk), 1)
    causal = key <= qry
    masked = jnp.full((blk, blk), MASKED, F32)

    def scores(kb, kind, lanes):
        k = k_ref[0, pl.ds(pl.multiple_of(kb * blk, blk), blk), :]
        if kind == "diag0":
            st = jnp.dot(k, qt[:, blk:], preferred_element_type=F32)
            return jnp.concatenate([masked, jnp.where(causal, st, MASKED)], axis=1)
        st = jnp.dot(k, qt[:, lanes], preferred_element_type=F32)
        if kind == "diag1":
            return jnp.concatenate([jnp.where(causal, st[:, :blk], MASKED), st[:, blk:]], axis=1)
        return st

    _attn_pipeline(qi, scores, vt_ref, *scratch)
    acc = scratch[-1][...]
    ot = acc[:HEAD_DIM] / acc[HEAD_DIM:HEAD_DIM + 1]
    o_ref[0] = (ot.T * gate_ref[0]).astype(o_ref.dtype)


def _fox_attn(q, k, v, gate, blk):
    b, s, _ = v.shape
    qw = Q_PER_K * blk
    return pl.pallas_call(
        functools.partial(_fox_attn_kernel, blk=blk),
        grid=(b, N_HEADS, s // qw),
        in_specs=[
            pl.BlockSpec((1, qw, AUG), lambda bi, h, qi: (bi, qi, h)),
            pl.BlockSpec((1, s, AUG), lambda bi, h, qi: (bi, 0, h)),
            pl.BlockSpec((1, s, HEAD_DIM), lambda bi, h, qi: (bi, 0, h)),
            pl.BlockSpec((1, qw, HEAD_DIM), lambda bi, h, qi: (bi, qi, h)),
        ],
        out_specs=pl.BlockSpec((1, qw, HEAD_DIM), lambda bi, h, qi: (bi, qi, h)),
        out_shape=jax.ShapeDtypeStruct((b, s, GROUP_WIDTH), BF16),
        scratch_shapes=_attn_scratch(s // blk, blk, qw),
        compiler_params=_cparams(("arbitrary", "arbitrary", "arbitrary")),
        name="fox_attn",
    )(q, k, v, gate)


def _diff_attn_kernel(q_ref, k_ref, v_ref, gate_ref, table_ref, lam_ref, onorm_ref, o_ref, bias_ref, vt_ref,
                      *scratch, blk, lambda_init):
    qi = pl.program_id(2)

    @pl.when(qi == 0)
    def _():
        _fill_vt(v_ref, vt_ref, blk)
        shifted = pltpu.roll(jnp.broadcast_to(table_ref[0], (blk, 2 * blk)), 0, 1, stride=1, stride_axis=0)
        key = lax.broadcasted_iota(jnp.int32, (blk, blk), 0)
        qry = lax.broadcasted_iota(jnp.int32, (blk, blk), 1)
        bias_ref[0] = jnp.where(key <= qry, shifted[:, :blk], MASKED)
        bias_ref[1] = shifted[:, blk:]

    qw = Q_PER_K * blk
    qt = q_ref[0].astype(F32).T
    dim = lax.broadcasted_iota(jnp.int32, qt.shape, 0)
    qzt = jnp.concatenate([jnp.where(dim < DIFF_QK, qt, 0.0), jnp.where(dim >= DIFF_QK, qt, 0.0)],
                          axis=1).astype(BF16)
    masked = jnp.full((blk, blk), MASKED, F32)

    def scores(kb, kind, lanes):
        k = k_ref[0, pl.ds(pl.multiple_of(kb * blk, blk), blk), :]
        if kind == "diag0":
            second = jnp.concatenate([qzt[:, blk:qw], qzt[:, qw + blk:]], axis=1)
            st = jnp.dot(k, second, preferred_element_type=F32)
            diag = bias_ref[0]
            return jnp.concatenate([masked, st[:, :blk] + diag, masked, st[:, blk:] + diag], axis=1)
        st = jnp.dot(k, qzt[:, lanes], preferred_element_type=F32)
        if kind == "far":
            return st
        below = bias_ref[1]
        parts = [st[:, j * blk:(j + 1) * blk] for j in range(2 * Q_PER_K)]
        if kind == "diag1":
            diag = bias_ref[0]
            parts = [parts[0] + diag, parts[1] + below, parts[2] + diag, parts[3] + below]
        else:
            parts = [parts[0] + below, parts[1], parts[2] + below, parts[3]]
        return jnp.concatenate(parts, axis=1)

    _attn_pipeline(qi, scores, vt_ref, *scratch)

    lam_p = lam_ref[...]
    lam = (jnp.exp(jnp.sum(lam_p[0:1] * lam_p[1:2], axis=-1, keepdims=True))
           - jnp.exp(jnp.sum(lam_p[2:3] * lam_p[3:4], axis=-1, keepdims=True)) + lambda_init)
    acc = scratch[-1][...]
    ot = acc[:HEAD_DIM] / acc[HEAD_DIM:HEAD_DIM + 1]
    o = (ot[:, :qw] - lam * ot[:, qw:]).T
    ms = jnp.mean(o * o, axis=-1, keepdims=True)
    o = o * lax.rsqrt(ms + EPS) * onorm_ref[...] * (1.0 - lambda_init)
    o_ref[0] = (o * gate_ref[0]).astype(o_ref.dtype)


def _diff_attn(q, k, v, gate, bias_table, lam_params, onorm, blk, lambda_init):
    b, s, _ = v.shape
    qw = Q_PER_K * blk
    return pl.pallas_call(
        functools.partial(_diff_attn_kernel, blk=blk, lambda_init=lambda_init),
        grid=(b, N_HEADS, s // qw),
        in_specs=[
            pl.BlockSpec((1, qw, HEAD_DIM), lambda bi, h, qi: (bi, qi, h)),
            pl.BlockSpec((1, s, HEAD_DIM), lambda bi, h, qi: (bi, 0, h)),
            pl.BlockSpec((1, s, HEAD_DIM), lambda bi, h, qi: (bi, 0, h)),
            pl.BlockSpec((1, qw, HEAD_DIM), lambda bi, h, qi: (bi, qi, h)),
            pl.BlockSpec((1, 1, 2 * blk), lambda bi, h, qi: (h, 0, 0)),
            pl.BlockSpec((4, DIFF_QK), lambda bi, h, qi: (0, 0)),
            pl.BlockSpec((1, HEAD_DIM), lambda bi, h, qi: (0, 0)),
        ],
        out_specs=pl.BlockSpec((1, qw, HEAD_DIM), lambda bi, h, qi: (bi, qi, h)),
        out_shape=jax.ShapeDtypeStruct((b, s, GROUP_WIDTH), BF16),
        scratch_shapes=[pltpu.VMEM((2, blk, blk), F32)] + _attn_scratch(s // blk, blk, 2 * qw),
        compiler_params=_cparams(("arbitrary", "arbitrary", "arbitrary")),
        name="diff_attn",
    )(q, k, v, gate, bias_table, lam_params, onorm)


def _out_ple_kernel(x_ref, ma_ref, mb_ref, p_ref, wo_ref, wp_ref, gn_ref, wg_ref, o_ref):
    half = ma_ref.shape[1]
    h = x_ref[...]
    h = h + jnp.dot(ma_ref[...], wo_ref[:half, :], preferred_element_type=F32)
    h = h + jnp.dot(mb_ref[...], wo_ref[half:, :], preferred_element_type=F32)
    ms = jnp.mean(h * h, axis=-1, keepdims=True)
    hn = (h * lax.rsqrt(ms + EPS) * gn_ref[...]).astype(BF16)
    zg = jnp.dot(hn, wg_ref[...], preferred_element_type=F32)
    g = 1.0 / (1.0 + jnp.exp(-zg))
    e = jnp.dot(p_ref[...].astype(BF16), wp_ref[...], preferred_element_type=F32)
    o_ref[...] = h + g * e


def _out_ple(x2, mix_a, mix_b, p2, w_out, ple_proj, gate_norm, ple_gate, tm):
    m, d = x2.shape
    half = mix_a.shape[1]
    pdim = p2.shape[1]
    const = lambda i: (0, 0)
    single = dict(pipeline_mode=pl.Buffered(1))
    return pl.pallas_call(
        _out_ple_kernel,
        grid=(m // tm,),
        in_specs=[
            pl.BlockSpec((tm, d), lambda i: (i, 0)),
            pl.BlockSpec((tm, half), lambda i: (i, 0)),
            pl.BlockSpec((tm, half), lambda i: (i, 0)),
            pl.BlockSpec((tm, pdim), lambda i: (i, 0)),
            pl.BlockSpec((2 * half, d), const, **single),
            pl.BlockSpec((pdim, d), const, **single),
            pl.BlockSpec((1, d), const),
            pl.BlockSpec((d, d), const, **single),
        ],
        out_specs=pl.BlockSpec((tm, d), lambda i: (i, 0)),
        out_shape=jax.ShapeDtypeStruct((m, d), F32),
        compiler_params=_cparams(("arbitrary",)),
        name="out_ple",
    )(x2, mix_a, mix_b, p2, w_out, ple_proj, gate_norm, ple_gate)


def _t5_bucket(n):
    max_exact = N_BUCKETS // 2
    nf = jnp.maximum(n, 1).astype(F32)
    large = max_exact + (jnp.log(nf / max_exact) / math.log(MAX_DISTANCE / max_exact)
                         * (N_BUCKETS - max_exact)).astype(jnp.int32)
    large = jnp.minimum(large, N_BUCKETS - 1)
    return jnp.where(n < max_exact, n, large)


def _bias_table(rel_bias, blk):
    assert blk >= MAX_DISTANCE
    table = rel_bias.astype(F32)[_t5_bucket(jnp.arange(2 * blk))].T
    table = (table - rel_bias.astype(F32)[N_BUCKETS - 1][:, None]) * LOG2E
    return table[:, None, :]


def kernel(x, p, attn_norm, w_in, b_forget, fox_q_norm, fox_k_norm, diff_q_norm, diff_k_norm,
           lambda_q1, lambda_k1, lambda_q2, lambda_k2, diff_out_norm, w_out, rel_bias,
           ple_proj, ple_gate_norm, ple_gate):
    b, s, d = x.shape
    depth = w_in.shape[0]
    gw = GROUP_WIDTH
    blk = 512
    tm = 1024
    bias_table = _bias_table(rel_bias, blk)

    h = x.reshape(b * s, d)
    for i in range(depth):
        lambda_init = 0.8 - 0.6 * math.exp(-0.3 * i)
        w = w_in[i].astype(BF16)
        w_f = jnp.pad(w[:, 8 * gw:], ((0, 0), (0, LANES - N_HEADS)))
        b_f = jnp.pad(b_forget[i].astype(F32), (0, LANES - N_HEADS)).reshape(1, LANES)
        u, c2 = _norm_forget(h, attn_norm[i].astype(F32).reshape(1, d), w_f, b_f, s, 512)

        sec = lambda j: w[:, j * gw:(j + 1) * gw]
        row = lambda a: a.astype(F32).reshape(1, -1)
        vec_spec = lambda n: pl.BlockSpec((1, n), lambda r: (0, 0))
        c_spec = pl.BlockSpec((tm, LANES), lambda r: (r, 0))
        fox_scale = HEAD_DIM ** -0.5 * LOG2E
        diff_scale = DIFF_QK ** -0.5 * LOG2E

        qa = _proj(functools.partial(_proj_fox_kernel, is_q=True, scale=fox_scale), u, sec(0),
                   [row(fox_q_norm[i]), c2], [vec_spec(HEAD_DIM), c_spec], N_HEADS * AUG, BF16, tm, "proj_fox_q")
        ka = _proj(functools.partial(_proj_fox_kernel, is_q=False, scale=1.0), u, sec(1),
                   [row(fox_k_norm[i]), c2], [vec_spec(HEAD_DIM), c_spec], N_HEADS * AUG, BF16, tm, "proj_fox_k")
        va = _proj(_proj_plain_kernel, u, sec(2), [], [], gw, BF16, tm, "proj_fox_v")
        ga = _proj(_proj_silu_kernel, u, sec(3), [], [], gw, F32, tm, "proj_fox_gate")
        dq_gain = row(jnp.tile(diff_q_norm[i], 2))
        dk_gain = row(jnp.tile(diff_k_norm[i], 2))
        qb = _proj(functools.partial(_proj_diff_kernel, scale=diff_scale), u, sec(4),
                   [dq_gain], [vec_spec(HEAD_DIM)], gw, BF16, tm, "proj_diff_q")
        kb = _proj(functools.partial(_proj_diff_kernel, scale=1.0), u, sec(5),
                   [dk_gain], [vec_spec(HEAD_DIM)], gw, BF16, tm, "proj_diff_k")
        vb = _proj(_proj_plain_kernel, u, sec(6), [], [], gw, BF16, tm, "proj_diff_v")
        gb = _proj(_proj_silu_kernel, u, sec(7), [], [], gw, F32, tm, "proj_diff_gate")

        to3 = lambda a: a.reshape(b, s, a.shape[-1])
        mix_a = _fox_attn(to3(qa), to3(ka), to3(va), to3(ga), blk)
        lam_params = jnp.stack([lambda_q1[i], lambda_k1[i], lambda_q2[i], lambda_k2[i]]).astype(F32)
        mix_b = _diff_attn(to3(qb), to3(kb), to3(vb), to3(gb), bias_table, lam_params,
                           row(diff_out_norm[i]), blk, lambda_init)

        h = _out_ple(h, mix_a.reshape(b * s, gw), mix_b.reshape(b * s, gw),
                     p[i].reshape(b * s, -1), w_out[i].astype(BF16), ple_proj[i].astype(BF16),
                     row(ple_gate_norm[i]), ple_gate[i].astype(BF16), 256)
    return h.reshape(b, s, d)
```

```python
import functools
import math

import jax
import jax.numpy as jnp
from jax import lax
from jax.experimental import pallas as pl
from jax.experimental.pallas import tpu as pltpu

F32 = jnp.float32
BF16 = jnp.bfloat16

HEAD_DIM = 128
N_HEADS = 8
DIFF_QK = HEAD_DIM // 2
GROUP_WIDTH = N_HEADS * HEAD_DIM
N_BUCKETS = 32
MAX_DISTANCE = 128
EPS = 1e-6
LOG2E = math.log2(math.e)
MASKED = -3e38
M_FLOOR = -(2.0 ** 100)
Q_PER_K = 2
QUERY_CHUNK = 256
LANES = 128
AUG = 2 * HEAD_DIM
VMEM_LIMIT = 56 * 1024 * 1024


def _cparams(sem):
    return pltpu.CompilerParams(dimension_semantics=sem, vmem_limit_bytes=VMEM_LIMIT)


def _split3(v):
    p0 = v.astype(BF16).astype(F32)
    r1 = v - p0
    p1 = r1.astype(BF16).astype(F32)
    p2 = (r1 - p1).astype(BF16).astype(F32)
    return p0, p1, p2


def _norm_forget_kernel(x_ref, g_ref, wf_ref, bf_ref, u_ref, c_ref, carry_ref, *, tiles_per_seq):
    i = pl.program_id(0)
    x = x_ref[...]
    tm = x.shape[0]
    ms = jnp.mean(x * x, axis=-1, keepdims=True)
    u = (x * lax.rsqrt(ms + EPS) * g_ref[...]).astype(BF16)
    u_ref[...] = u

    z = jnp.dot(u, wf_ref[...], preferred_element_type=F32) + bf_ref[...]
    lf = (jnp.minimum(z, 0.0) - jnp.log(1.0 + jnp.exp(-jnp.abs(z)))) * LOG2E

    @pl.when(i % tiles_per_seq == 0)
    def _():
        carry_ref[...] = jnp.zeros_like(carry_ref)

    row = lax.broadcasted_iota(jnp.int32, (tm, tm), 0)
    col = lax.broadcasted_iota(jnp.int32, (tm, tm), 1)
    tri = jnp.where(col <= row, 1.0, 0.0).astype(BF16)
    c = carry_ref[...]
    for part in _split3(lf):
        c = c + jnp.dot(tri, part.astype(BF16), preferred_element_type=F32)
    c_ref[...] = c
    carry_ref[...] = c[tm - 1:tm, :]


def _norm_forget(x2, gain, w_f, b_f, seq, tm):
    m, d = x2.shape
    kern = functools.partial(_norm_forget_kernel, tiles_per_seq=seq // tm)
    return pl.pallas_call(
        kern,
        grid=(m // tm,),
        in_specs=[
            pl.BlockSpec((tm, d), lambda i: (i, 0)),
            pl.BlockSpec((1, d), lambda i: (0, 0)),
            pl.BlockSpec((d, LANES), lambda i: (0, 0)),
            pl.BlockSpec((1, LANES), lambda i: (0, 0)),
        ],
        out_specs=[
            pl.BlockSpec((tm, d), lambda i: (i, 0)),
            pl.BlockSpec((tm, LANES), lambda i: (i, 0)),
        ],
        out_shape=[
            jax.ShapeDtypeStruct((m, d), BF16),
            jax.ShapeDtypeStruct((m, LANES), F32),
        ],
        scratch_shapes=[pltpu.VMEM((1, LANES), F32)],
        compiler_params=_cparams(("arbitrary",)),
        name="norm_forget",
    )(x2, gain, w_f, b_f)


def _proj_plain_kernel(u_ref, w_ref, o_ref):
    y = jnp.dot(u_ref[...], w_ref[...], preferred_element_type=F32)
    o_ref[...] = y.astype(o_ref.dtype)


def _proj_silu_kernel(u_ref, w_ref, o_ref):
    y = jnp.dot(u_ref[...], w_ref[...], preferred_element_type=F32)
    o_ref[...] = (y * (1.0 / (1.0 + jnp.exp(-y)))).astype(o_ref.dtype)


def _proj_fox_kernel(u_ref, w_ref, g_ref, c_ref, o_ref, *, is_q, scale):
    y = jnp.dot(u_ref[...], w_ref[...], preferred_element_type=F32)
    tm = y.shape[0]
    lane = lax.broadcasted_iota(jnp.int32, (tm, LANES), 1)
    gain = g_ref[...] * scale
    c_all = c_ref[...]
    for h in range(N_HEADS):
        yh = y[:, h * HEAD_DIM:(h + 1) * HEAD_DIM]
        ms = jnp.mean(yh * yh, axis=-1, keepdims=True)
        o_ref[:, h * AUG:h * AUG + HEAD_DIM] = (yh * lax.rsqrt(ms + EPS) * gain).astype(BF16)
        ch = jnp.broadcast_to(c_all[:, h:h + 1], (tm, LANES))
        if not is_q:
            ch = -ch
        p0, p1, p2 = _split3(ch)
        first = 0 if is_q else 3
        ones_at = 3 if is_q else 0
        aug = jnp.where(lane == first, p0, jnp.where(lane == first + 1, p1, jnp.where(lane == first + 2, p2, 0.0)))
        aug = jnp.where((lane >= ones_at) & (lane < ones_at + 3), 1.0, aug)
        o_ref[:, h * AUG + HEAD_DIM:(h + 1) * AUG] = aug.astype(BF16)


def _proj_diff_kernel(u_ref, w_ref, g_ref, o_ref, *, scale):
    y = jnp.dot(u_ref[...], w_ref[...], preferred_element_type=F32)
    tm = y.shape[0]
    lane = lax.broadcasted_iota(jnp.int32, (tm, LANES), 1)
    lo = lane < DIFF_QK
    gain = g_ref[...] * scale
    for h in range(N_HEADS):
        yh = y[:, h * HEAD_DIM:(h + 1) * HEAD_DIM]
        sq = yh * yh
        s_lo = jnp.sum(jnp.where(lo, sq, 0.0), axis=-1, keepdims=True)
        s_all = jnp.sum(sq, axis=-1, keepdims=True)
        ms = jnp.where(lo, s_lo, s_all - s_lo) * (1.0 / DIFF_QK)
        o_ref[:, h * HEAD_DIM:(h + 1) * HEAD_DIM] = (yh * lax.rsqrt(ms + EPS) * gain).astype(BF16)


def _proj(kern, u, w_sec, extra, extra_specs, out_width, out_dtype, tm, name):
    m, d = u.shape
    n = w_sec.shape[1]
    return pl.pallas_call(
        kern,
        grid=(m // tm,),
        in_specs=[
            pl.BlockSpec((tm, d), lambda i: (i, 0)),
            pl.BlockSpec((d, n), lambda i: (0, 0)),
        ] + extra_specs,
        out_specs=pl.BlockSpec((tm, out_width), lambda i: (i, 0)),
        out_shape=jax.ShapeDtypeStruct((m, out_width), out_dtype),
        compiler_params=_cparams(("arbitrary",)),
        name=name,
    )(u, w_sec, *extra)


V_ROWS = HEAD_DIM + 16


def _fill_vt(v_ref, vt_ref, blk):
    tail_row = lax.broadcasted_iota(jnp.int32, (V_ROWS - HEAD_DIM, blk), 0)
    tail = jnp.where(tail_row == 0, 1.0, 0.0).astype(BF16)

    @pl.loop(0, vt_ref.shape[0])
    def _(j):
        start = pl.multiple_of(j * blk, blk)
        vt_ref[j, :HEAD_DIM, :] = v_ref[0, pl.ds(start, blk), :].astype(F32).T.astype(BF16)
        vt_ref[j, HEAD_DIM:, :] = tail


def _attn_scratch(n_blocks, blk, width):
    return [
        pltpu.VMEM((n_blocks, V_ROWS, blk), BF16),
        pltpu.VMEM((blk, width), BF16),
        pltpu.VMEM((1, width), F32),
        pltpu.VMEM((blk, width), BF16),
        pltpu.VMEM((1, width), F32),
        pltpu.VMEM((1, width), F32),
        pltpu.VMEM((V_ROWS, width), F32),
    ]


def _query_chunks(blk, n_maps):
    qw = Q_PER_K * blk
    chunks = [slice(lo, lo + QUERY_CHUNK) for lo in range(0, n_maps * qw, QUERY_CHUNK)]
    return chunks, [c for c in chunks if c.start % qw >= blk]


def _attn_pipeline(qi, scores, all_chunks, hi_chunks, vt_ref, s_ref, mx_ref, p_ref, al_ref, m_ref, acc_ref):
    n = Q_PER_K * (qi + 1)

    def key_block(t):
        return jnp.maximum(n - 1 - t, 0)

    def qk(t, kind, lanes):
        st = scores(key_block(t), kind, lanes)
        s_ref[:, lanes] = st.astype(BF16)
        mx_ref[:, lanes] = jnp.max(st, axis=0, keepdims=True)

    def softmax(lanes):
        m_old = m_ref[:, lanes]
        m_new = jnp.maximum(m_old, mx_ref[:, lanes]).astype(BF16)
        m_ref[:, lanes] = m_new.astype(F32)
        al_ref[:, lanes] = jnp.exp2(m_old - m_new.astype(F32))
        p_ref[:, lanes] = jnp.exp2(s_ref[:, lanes] - m_new)

    def pv(t, lanes):
        acc_ref[:, lanes] = al_ref[:, lanes] * acc_ref[:, lanes] + jnp.dot(
            vt_ref[key_block(t)], p_ref[:, lanes], preferred_element_type=F32)

    def step(t, kind, pv_chunks=all_chunks, with_qk=True):
        for lanes in all_chunks:
            if lanes in pv_chunks:
                pv(t - 1, lanes)
            softmax(lanes)
            if with_qk:
                qk(t + 1, kind, lanes)

    m_ref[...] = jnp.full(m_ref.shape, M_FLOOR, F32)
    acc_ref[...] = jnp.zeros(acc_ref.shape, F32)

    for lanes in hi_chunks:
        qk(0, "diag0", lanes)
    for lanes in all_chunks:
        if lanes in hi_chunks:
            softmax(lanes)
        qk(1, "diag1", lanes)
    step(1, "near", pv_chunks=hi_chunks)

    @pl.loop(0, jnp.maximum(qi - 1, 0))
    def _(u):
        step(2 * u + 2, "far")
        step(2 * u + 3, "far")

    @pl.when(qi >= 1)
    def _():
        step(n - 2, "far")
        step(n - 1, "far", with_qk=False)

    for lanes in all_chunks:
        pv(n - 1, lanes)


def _fox_attn_kernel(q_ref, k_ref, v_ref, gate_ref, o_ref, vt_ref, *scratch, blk):
    qi = pl.program_id(2)

    @pl.when(qi == 0)
    def _():
        _fill_vt(v_ref, vt_ref, blk)

    qt = q_ref[0].astype(F32).T.astype(BF16)
    key = lax.broadcasted_iota(jnp.int32, (blk, blk), 0)
    qry = lax.broadcasted_iota(jnp.int32, (blk, blk), 1)
    causal = key <= qry

    def scores(kb, kind, lanes):
        k = k_ref[0, pl.ds(pl.multiple_of(kb * blk, blk), blk), :]
        st = jnp.dot(k, qt[:, lanes], preferred_element_type=F32)
        second_half = lanes.start >= blk
        if (kind == "diag0" and second_half) or (kind == "diag1" and not second_half):
            within = slice(lanes.start % blk, lanes.start % blk + QUERY_CHUNK)
            return jnp.where(causal[:, within], st, MASKED)
        assert kind != "diag0"
        return st

    all_chunks, hi_chunks = _query_chunks(blk, 1)
    _attn_pipeline(qi, scores, all_chunks, hi_chunks, vt_ref, *scratch)
    acc = scratch[-1][...]
    ot = acc[:HEAD_DIM] / acc[HEAD_DIM:HEAD_DIM + 1]
    o_ref[0] = (ot.T * gate_ref[0]).astype(o_ref.dtype)


def _fox_attn(q, k, v, gate, blk):
    b, s, _ = v.shape
    qw = Q_PER_K * blk
    return pl.pallas_call(
        functools.partial(_fox_attn_kernel, blk=blk),
        grid=(b, N_HEADS, s // qw),
        in_specs=[
            pl.BlockSpec((1, qw, AUG), lambda bi, h, qi: (bi, qi, h)),
            pl.BlockSpec((1, s, AUG), lambda bi, h, qi: (bi, 0, h)),
            pl.BlockSpec((1, s, HEAD_DIM), lambda bi, h, qi: (bi, 0, h)),
            pl.BlockSpec((1, qw, HEAD_DIM), lambda bi, h, qi: (bi, qi, h)),
        ],
        out_specs=pl.BlockSpec((1, qw, HEAD_DIM), lambda bi, h, qi: (bi, qi, h)),
        out_shape=jax.ShapeDtypeStruct((b, s, GROUP_WIDTH), BF16),
        scratch_shapes=_attn_scratch(s // blk, blk, qw),
        compiler_params=_cparams(("arbitrary", "arbitrary", "arbitrary")),
        name="fox_attn",
    )(q, k, v, gate)


def _diff_attn_kernel(q_ref, k_ref, v_ref, gate_ref, table_ref, lam_ref, onorm_ref, o_ref, bias_ref, vt_ref,
                      *scratch, blk, lambda_init):
    qi = pl.program_id(2)

    @pl.when(qi == 0)
    def _():
        _fill_vt(v_ref, vt_ref, blk)
        shifted = pltpu.roll(jnp.broadcast_to(table_ref[0], (blk, 2 * blk)), 0, 1, stride=1, stride_axis=0)
        key = lax.broadcasted_iota(jnp.int32, (blk, blk), 0)
        qry = lax.broadcasted_iota(jnp.int32, (blk, blk), 1)
        bias_ref[0] = jnp.where(key <= qry, shifted[:, :blk], MASKED)
        bias_ref[1] = shifted[:, blk:]

    qw = Q_PER_K * blk
    qt = q_ref[0].astype(F32).T
    dim = lax.broadcasted_iota(jnp.int32, qt.shape, 0)
    qzt = jnp.concatenate([jnp.where(dim < DIFF_QK, qt, 0.0), jnp.where(dim >= DIFF_QK, qt, 0.0)],
                          axis=1).astype(BF16)
    def scores(kb, kind, lanes):
        k = k_ref[0, pl.ds(pl.multiple_of(kb * blk, blk), blk), :]
        st = jnp.dot(k, qzt[:, lanes], preferred_element_type=F32)
        second_half = lanes.start % qw >= blk
        within = slice(lanes.start % blk, lanes.start % blk + QUERY_CHUNK)
        if kind == "diag0":
            assert second_half
            return st + bias_ref[0, :, within]
        if kind == "diag1":
            return st + bias_ref[1 if second_half else 0, :, within]
        if kind == "near" and not second_half:
            return st + bias_ref[1, :, within]
        return st

    all_chunks, hi_chunks = _query_chunks(blk, 2)
    _attn_pipeline(qi, scores, all_chunks, hi_chunks, vt_ref, *scratch)

    lam_p = lam_ref[...]
    lam = (jnp.exp(jnp.sum(lam_p[0:1] * lam_p[1:2], axis=-1, keepdims=True))
           - jnp.exp(jnp.sum(lam_p[2:3] * lam_p[3:4], axis=-1, keepdims=True)) + lambda_init)
    acc = scratch[-1][...]
    ot = acc[:HEAD_DIM] / acc[HEAD_DIM:HEAD_DIM + 1]
    o = (ot[:, :qw] - lam * ot[:, qw:]).T
    ms = jnp.mean(o * o, axis=-1, keepdims=True)
    o = o * lax.rsqrt(ms + EPS) * onorm_ref[...] * (1.0 - lambda_init)
    o_ref[0] = (o * gate_ref[0]).astype(o_ref.dtype)


def _diff_attn(q, k, v, gate, bias_table, lam_params, onorm, blk, lambda_init):
    b, s, _ = v.shape
    qw = Q_PER_K * blk
    return pl.pallas_call(
        functools.partial(_diff_attn_kernel, blk=blk, lambda_init=lambda_init),
        grid=(b, N_HEADS, s // qw),
        in_specs=[
            pl.BlockSpec((1, qw, HEAD_DIM), lambda bi, h, qi: (bi, qi, h)),
            pl.BlockSpec((1, s, HEAD_DIM), lambda bi, h, qi: (bi, 0, h)),
            pl.BlockSpec((1, s, HEAD_DIM), lambda bi, h, qi: (bi, 0, h)),
            pl.BlockSpec((1, qw, HEAD_DIM), lambda bi, h, qi: (bi, qi, h)),
            pl.BlockSpec((1, 1, 2 * blk), lambda bi, h, qi: (h, 0, 0)),
            pl.BlockSpec((4, DIFF_QK), lambda bi, h, qi: (0, 0)),
            pl.BlockSpec((1, HEAD_DIM), lambda bi, h, qi: (0, 0)),
        ],
        out_specs=pl.BlockSpec((1, qw, HEAD_DIM), lambda bi, h, qi: (bi, qi, h)),
        out_shape=jax.ShapeDtypeStruct((b, s, GROUP_WIDTH), BF16),
        scratch_shapes=[pltpu.VMEM((2, blk, blk), F32)] + _attn_scratch(s // blk, blk, 2 * qw),
        compiler_params=_cparams(("arbitrary", "arbitrary", "arbitrary")),
        name="diff_attn",
    )(q, k, v, gate, bias_table, lam_params, onorm)


def _out_ple_kernel(x_ref, ma_ref, mb_ref, p_ref, wo_ref, wp_ref, gn_ref, wg_ref, o_ref):
    half = ma_ref.shape[1]
    h = x_ref[...]
    h = h + jnp.dot(ma_ref[...], wo_ref[:half, :], preferred_element_type=F32)
    h = h + jnp.dot(mb_ref[...], wo_ref[half:, :], preferred_element_type=F32)
    ms = jnp.mean(h * h, axis=-1, keepdims=True)
    hn = (h * lax.rsqrt(ms + EPS) * gn_ref[...]).astype(BF16)
    zg = jnp.dot(hn, wg_ref[...], preferred_element_type=F32)
    g = 1.0 / (1.0 + jnp.exp(-zg))
    e = jnp.dot(p_ref[...].astype(BF16), wp_ref[...], preferred_element_type=F32)
    o_ref[...] = h + g * e


def _out_ple(x2, mix_a, mix_b, p2, w_out, ple_proj, gate_norm, ple_gate, tm):
    m, d = x2.shape
    half = mix_a.shape[1]
    pdim = p2.shape[1]
    const = lambda i: (0, 0)
    single = dict(pipeline_mode=pl.Buffered(1))
    return pl.pallas_call(
        _out_ple_kernel,
        grid=(m // tm,),
        in_specs=[
            pl.BlockSpec((tm, d), lambda i: (i, 0)),
            pl.BlockSpec((tm, half), lambda i: (i, 0)),
            pl.BlockSpec((tm, half), lambda i: (i, 0)),
            pl.BlockSpec((tm, pdim), lambda i: (i, 0)),
            pl.BlockSpec((2 * half, d), const, **single),
            pl.BlockSpec((pdim, d), const, **single),
            pl.BlockSpec((1, d), const),
            pl.BlockSpec((d, d), const, **single),
        ],
        out_specs=pl.BlockSpec((tm, d), lambda i: (i, 0)),
        out_shape=jax.ShapeDtypeStruct((m, d), F32),
        compiler_params=_cparams(("arbitrary",)),
        name="out_ple",
    )(x2, mix_a, mix_b, p2, w_out, ple_proj, gate_norm, ple_gate)


def _t5_bucket(n):
    max_exact = N_BUCKETS // 2
    nf = jnp.maximum(n, 1).astype(F32)
    large = max_exact + (jnp.log(nf / max_exact) / math.log(MAX_DISTANCE / max_exact)
                         * (N_BUCKETS - max_exact)).astype(jnp.int32)
    large = jnp.minimum(large, N_BUCKETS - 1)
    return jnp.where(n < max_exact, n, large)


def _bias_table(rel_bias, blk):
    assert blk >= MAX_DISTANCE
    table = rel_bias.astype(F32)[_t5_bucket(jnp.arange(2 * blk))].T
    table = (table - rel_bias.astype(F32)[N_BUCKETS - 1][:, None]) * LOG2E
    return table[:, None, :]


def kernel(x, p, attn_norm, w_in, b_forget, fox_q_norm, fox_k_norm, diff_q_norm, diff_k_norm,
           lambda_q1, lambda_k1, lambda_q2, lambda_k2, diff_out_norm, w_out, rel_bias,
           ple_proj, ple_gate_norm, ple_gate):
    b, s, d = x.shape
    depth = w_in.shape[0]
    gw = GROUP_WIDTH
    blk = 512
    tm = 1024
    bias_table = _bias_table(rel_bias, blk)

    h = x.reshape(b * s, d)
    for i in range(depth):
        lambda_init = 0.8 - 0.6 * math.exp(-0.3 * i)
        w = w_in[i].astype(BF16)
        w_f = jnp.pad(w[:, 8 * gw:], ((0, 0), (0, LANES - N_HEADS)))
        b_f = jnp.pad(b_forget[i].astype(F32), (0, LANES - N_HEADS)).reshape(1, LANES)
        u, c2 = _norm_forget(h, attn_norm[i].astype(F32).reshape(1, d), w_f, b_f, s, 512)

        sec = lambda j: w[:, j * gw:(j + 1) * gw]
        row = lambda a: a.astype(F32).reshape(1, -1)
        vec_spec = lambda n: pl.BlockSpec((1, n), lambda r: (0, 0))
        c_spec = pl.BlockSpec((tm, LANES), lambda r: (r, 0))
        fox_scale = HEAD_DIM ** -0.5 * LOG2E
        diff_scale = DIFF_QK ** -0.5 * LOG2E

        qa = _proj(functools.partial(_proj_fox_kernel, is_q=True, scale=fox_scale), u, sec(0),
                   [row(fox_q_norm[i]), c2], [vec_spec(HEAD_DIM), c_spec], N_HEADS * AUG, BF16, tm, "proj_fox_q")
        ka = _proj(functools.partial(_proj_fox_kernel, is_q=False, scale=1.0), u, sec(1),
                   [row(fox_k_norm[i]), c2], [vec_spec(HEAD_DIM), c_spec], N_HEADS * AUG, BF16, tm, "proj_fox_k")
        va = _proj(_proj_plain_kernel, u, sec(2), [], [], gw, BF16, tm, "proj_fox_v")
        ga = _proj(_proj_silu_kernel, u, sec(3), [], [], gw, F32, tm, "proj_fox_gate")
        dq_gain = row(jnp.tile(diff_q_norm[i], 2))
        dk_gain = row(jnp.tile(diff_k_norm[i], 2))
        qb = _proj(functools.partial(_proj_diff_kernel, scale=diff_scale), u, sec(4),
                   [dq_gain], [vec_spec(HEAD_DIM)], gw, BF16, tm, "proj_diff_q")
        kb = _proj(functools.partial(_proj_diff_kernel, scale=1.0), u, sec(5),
                   [dk_gain], [vec_spec(HEAD_DIM)], gw, BF16, tm, "proj_diff_k")
        vb = _proj(_proj_plain_kernel, u, sec(6), [], [], gw, BF16, tm, "proj_diff_v")
        gb = _proj(_proj_silu_kernel, u, sec(7), [], [], gw, F32, tm, "proj_diff_gate")

        to3 = lambda a: a.reshape(b, s, a.shape[-1])
        mix_a = _fox_attn(to3(qa), to3(ka), to3(va), to3(ga), blk)
        lam_params = jnp.stack([lambda_q1[i], lambda_k1[i], lambda_q2[i], lambda_k2[i]]).astype(F32)
        mix_b = _diff_attn(to3(qb), to3(kb), to3(vb), to3(gb), bias_table, lam_params,
                           row(diff_out_norm[i]), blk, lambda_init)

        h = _out_ple(h, mix_a.reshape(b * s, gw), mix_b.reshape(b * s, gw),
                     p[i].reshape(b * s, -1), w_out[i].astype(BF16), ple_proj[i].astype(BF16),
                     row(ple_gate_norm[i]), ple_gate[i].astype(BF16), 256)
    return h.reshape(b, s, d)
```

```python
import functools
import math

import jax
import jax.numpy as jnp
from jax import lax
from jax.experimental import pallas as pl
from jax.experimental.pallas import tpu as pltpu

F32 = jnp.float32
BF16 = jnp.bfloat16

HEAD_DIM = 128
N_HEADS = 8
DIFF_QK = HEAD_DIM // 2
GROUP_WIDTH = N_HEADS * HEAD_DIM
N_BUCKETS = 32
MAX_DISTANCE = 128
EPS = 1e-6
LOG2E = math.log2(math.e)
MASKED = -3e38
M_FLOOR = -(2.0 ** 100)
Q_PER_K = 2
QUERY_CHUNK = 256
LANES = 128
AUG = 2 * HEAD_DIM
VMEM_LIMIT = 56 * 1024 * 1024


def _cparams(sem):
    return pltpu.CompilerParams(dimension_semantics=sem, vmem_limit_bytes=VMEM_LIMIT)


def _split3(v):
    p0 = v.astype(BF16).astype(F32)
    r1 = v - p0
    p1 = r1.astype(BF16).astype(F32)
    p2 = (r1 - p1).astype(BF16).astype(F32)
    return p0, p1, p2


def _norm_forget_kernel(x_ref, g_ref, wf_ref, bf_ref, u_ref, c_ref, carry_ref, *, tiles_per_seq):
    i = pl.program_id(0)
    x = x_ref[...]
    tm = x.shape[0]
    ms = jnp.mean(x * x, axis=-1, keepdims=True)
    u = (x * lax.rsqrt(ms + EPS) * g_ref[...]).astype(BF16)
    u_ref[...] = u

    z = jnp.dot(u, wf_ref[...], preferred_element_type=F32) + bf_ref[...]
    lf = (jnp.minimum(z, 0.0) - jnp.log(1.0 + jnp.exp(-jnp.abs(z)))) * LOG2E

    @pl.when(i % tiles_per_seq == 0)
    def _():
        carry_ref[...] = jnp.zeros_like(carry_ref)

    row = lax.broadcasted_iota(jnp.int32, (tm, tm), 0)
    col = lax.broadcasted_iota(jnp.int32, (tm, tm), 1)
    tri = jnp.where(col <= row, 1.0, 0.0).astype(BF16)
    c = carry_ref[...]
    for part in _split3(lf):
        c = c + jnp.dot(tri, part.astype(BF16), preferred_element_type=F32)
    c_ref[...] = c
    carry_ref[...] = c[tm - 1:tm, :]


def _norm_forget(x2, gain, w_f, b_f, seq, tm):
    m, d = x2.shape
    kern = functools.partial(_norm_forget_kernel, tiles_per_seq=seq // tm)
    return pl.pallas_call(
        kern,
        grid=(m // tm,),
        in_specs=[
            pl.BlockSpec((tm, d), lambda i: (i, 0)),
            pl.BlockSpec((1, d), lambda i: (0, 0)),
            pl.BlockSpec((d, LANES), lambda i: (0, 0)),
            pl.BlockSpec((1, LANES), lambda i: (0, 0)),
        ],
        out_specs=[
            pl.BlockSpec((tm, d), lambda i: (i, 0)),
            pl.BlockSpec((tm, LANES), lambda i: (i, 0)),
        ],
        out_shape=[
            jax.ShapeDtypeStruct((m, d), BF16),
            jax.ShapeDtypeStruct((m, LANES), F32),
        ],
        scratch_shapes=[pltpu.VMEM((1, LANES), F32)],
        compiler_params=_cparams(("arbitrary",)),
        name="norm_forget",
    )(x2, gain, w_f, b_f)


PROJ_CHUNK = 2 * HEAD_DIM


def _proj_chunks(u_ref, w_ref, wb_ref):
    @pl.when(pl.program_id(0) == 0)
    def _():
        wb_ref[...] = w_ref[...].astype(BF16)

    u = u_ref[...]
    for c in range(wb_ref.shape[1] // PROJ_CHUNK):
        cols = slice(c * PROJ_CHUNK, (c + 1) * PROJ_CHUNK)
        yield c * (PROJ_CHUNK // HEAD_DIM), jnp.dot(u, wb_ref[:, cols], preferred_element_type=F32)


def _proj_plain_kernel(u_ref, w_ref, o_ref, wb_ref):
    for h0, y in _proj_chunks(u_ref, w_ref, wb_ref):
        o_ref[:, h0 * HEAD_DIM:h0 * HEAD_DIM + PROJ_CHUNK] = y.astype(o_ref.dtype)


def _proj_silu_kernel(u_ref, w_ref, o_ref, wb_ref):
    for h0, y in _proj_chunks(u_ref, w_ref, wb_ref):
        o_ref[:, h0 * HEAD_DIM:h0 * HEAD_DIM + PROJ_CHUNK] = (y * (1.0 / (1.0 + jnp.exp(-y)))).astype(o_ref.dtype)


def _proj_heads_kernel(u_ref, w_ref, g_ref, o_ref, wb_ref, *, scale):
    gain = g_ref[...] * scale
    for h0, y in _proj_chunks(u_ref, w_ref, wb_ref):
        for j in range(PROJ_CHUNK // HEAD_DIM):
            h = h0 + j
            yh = y[:, j * HEAD_DIM:(j + 1) * HEAD_DIM]
            ms = jnp.mean(yh * yh, axis=-1, keepdims=True)
            o_ref[:, h * HEAD_DIM:(h + 1) * HEAD_DIM] = (yh * lax.rsqrt(ms + EPS) * gain).astype(BF16)


def _proj_fox_key_kernel(u_ref, w_ref, g_ref, c_ref, o_ref, wb_ref):
    tm = u_ref.shape[0]
    lane = lax.broadcasted_iota(jnp.int32, (tm, LANES), 1)
    gain = g_ref[...]
    c_all = c_ref[...]
    for h0, y in _proj_chunks(u_ref, w_ref, wb_ref):
        for j in range(PROJ_CHUNK // HEAD_DIM):
            h = h0 + j
            yh = y[:, j * HEAD_DIM:(j + 1) * HEAD_DIM]
            ms = jnp.mean(yh * yh, axis=-1, keepdims=True)
            o_ref[:, h * AUG:h * AUG + HEAD_DIM] = (yh * lax.rsqrt(ms + EPS) * gain).astype(BF16)
            p0, p1, p2 = _split3(-jnp.broadcast_to(c_all[:, h:h + 1], (tm, LANES)))
            aug = jnp.where(lane == 3, p0, jnp.where(lane == 4, p1, jnp.where(lane == 5, p2, 0.0)))
            aug = jnp.where(lane < 3, 1.0, aug)
            o_ref[:, h * AUG + HEAD_DIM:(h + 1) * AUG] = aug.astype(BF16)


def _proj_diff_kernel(u_ref, w_ref, g_ref, o_ref, wb_ref, *, scale):
    tm = u_ref.shape[0]
    lane = lax.broadcasted_iota(jnp.int32, (tm, LANES), 1)
    lo = lane < DIFF_QK
    gain = g_ref[...] * scale
    for h0, y in _proj_chunks(u_ref, w_ref, wb_ref):
        for j in range(PROJ_CHUNK // HEAD_DIM):
            h = h0 + j
            yh = y[:, j * HEAD_DIM:(j + 1) * HEAD_DIM]
            sq = yh * yh
            s_lo = jnp.sum(jnp.where(lo, sq, 0.0), axis=-1, keepdims=True)
            s_all = jnp.sum(sq, axis=-1, keepdims=True)
            ms = jnp.where(lo, s_lo, s_all - s_lo) * (1.0 / DIFF_QK)
            o_ref[:, h * HEAD_DIM:(h + 1) * HEAD_DIM] = (yh * lax.rsqrt(ms + EPS) * gain).astype(BF16)


def _proj(kern, u, w, section, extra, extra_specs, out_width, out_dtype, tm, name):
    m, d = u.shape
    n = GROUP_WIDTH
    return pl.pallas_call(
        kern,
        grid=(m // tm,),
        in_specs=[
            pl.BlockSpec((tm, d), lambda i: (i, 0)),
            pl.BlockSpec((d, n), lambda i: (0, section), pipeline_mode=pl.Buffered(1)),
        ] + extra_specs,
        out_specs=pl.BlockSpec((tm, out_width), lambda i: (i, 0)),
        out_shape=jax.ShapeDtypeStruct((m, out_width), out_dtype),
        scratch_shapes=[pltpu.VMEM((d, n), BF16)],
        compiler_params=_cparams(("arbitrary",)),
        name=name,
    )(u, w, *extra)


V_ROWS = HEAD_DIM + 16


def _fill_vt(v_ref, vt_ref, blk):
    tail_row = lax.broadcasted_iota(jnp.int32, (V_ROWS - HEAD_DIM, blk), 0)
    tail = jnp.where(tail_row == 0, 1.0, 0.0).astype(BF16)

    @pl.loop(0, vt_ref.shape[0])
    def _(j):
        start = pl.multiple_of(j * blk, blk)
        vt_ref[j, :HEAD_DIM, :] = v_ref[0, pl.ds(start, blk), :].astype(F32).T.astype(BF16)
        vt_ref[j, HEAD_DIM:, :] = tail


def _attn_scratch(n_blocks, blk, width):
    return [
        pltpu.VMEM((n_blocks, V_ROWS, blk), BF16),
        pltpu.VMEM((blk, width), BF16),
        pltpu.VMEM((1, width), F32),
        pltpu.VMEM((2, blk, width), BF16),
        pltpu.VMEM((2, 1, width), F32),
        pltpu.VMEM((1, width), F32),
        pltpu.VMEM((V_ROWS, width), F32),
    ]


def _query_chunks(blk, n_maps):
    qw = Q_PER_K * blk
    chunks = [slice(lo, lo + QUERY_CHUNK) for lo in range(0, n_maps * qw, QUERY_CHUNK)]
    return chunks, [c for c in chunks if c.start % qw >= blk]


def _attn_pipeline(qi, scores, all_chunks, hi_chunks, vt_ref, s_ref, mx_ref, p_ref, al_ref, m_ref, acc_ref):
    n = Q_PER_K * (qi + 1)

    def key_block(t):
        return jnp.maximum(n - 1 - t, 0)

    def qk(t, kind, lanes):
        st = scores(key_block(t), kind, lanes)
        s_ref[:, lanes] = st.astype(BF16)
        mx_ref[:, lanes] = jnp.max(st, axis=0, keepdims=True)

    def softmax(slot, lanes):
        m_old = m_ref[:, lanes]
        m_new = jnp.maximum(m_old, mx_ref[:, lanes]).astype(BF16)
        m_ref[:, lanes] = m_new.astype(F32)
        al_ref[slot, :, lanes] = jnp.exp2(m_old - m_new.astype(F32))
        p_ref[slot, :, lanes] = jnp.exp2(s_ref[:, lanes] - m_new)

    def pv(t, slot, lanes):
        acc_ref[:, lanes] = al_ref[slot, :, lanes] * acc_ref[:, lanes] + jnp.dot(
            vt_ref[key_block(t)], p_ref[slot, :, lanes], preferred_element_type=F32)

    def step(t, slot, kind, pv_chunks=all_chunks, with_qk=True):
        for lanes in all_chunks:
            softmax(slot, lanes)
            if lanes in pv_chunks:
                pv(t - 1, 1 - slot, lanes)
            if with_qk:
                qk(t + 1, kind, lanes)

    m_ref[...] = jnp.full(m_ref.shape, M_FLOOR, F32)
    acc_ref[...] = jnp.zeros(acc_ref.shape, F32)

    for lanes in hi_chunks:
        qk(0, "diag0", lanes)
    for lanes in all_chunks:
        if lanes in hi_chunks:
            softmax(0, lanes)
        qk(1, "diag1", lanes)
    step(1, 1, "near", pv_chunks=hi_chunks)

    @pl.loop(0, jnp.maximum(qi - 1, 0))
    def _(u):
        step(2 * u + 2, 0, "far")
        step(2 * u + 3, 1, "far")

    @pl.when(qi >= 1)
    def _():
        step(n - 2, 0, "far")
        step(n - 1, 1, "far", with_qk=False)

    for lanes in all_chunks:
        pv(n - 1, 1, lanes)


def _fox_attn_kernel(q_ref, c_ref, k_ref, v_ref, gate_ref, o_ref, vt_ref, *scratch, blk):
    qi = pl.program_id(2)

    @pl.when(qi == 0)
    def _():
        _fill_vt(v_ref, vt_ref, blk)

    qw = Q_PER_K * blk
    p0, p1, p2 = _split3(c_ref[0, 0])
    dim = lax.broadcasted_iota(jnp.int32, (AUG - HEAD_DIM, qw), 0)
    decay = jnp.where(dim == 0, p0, jnp.where(dim == 1, p1, jnp.where(dim == 2, p2, 0.0)))
    decay = jnp.where((dim >= 3) & (dim < 6), 1.0, decay)
    qt = jnp.concatenate([q_ref[0].astype(F32).T, decay], axis=0).astype(BF16)
    key = lax.broadcasted_iota(jnp.int32, (blk, blk), 0)
    qry = lax.broadcasted_iota(jnp.int32, (blk, blk), 1)
    causal = key <= qry

    def scores(kb, kind, lanes):
        k = k_ref[0, pl.ds(pl.multiple_of(kb * blk, blk), blk), :]
        st = jnp.dot(k, qt[:, lanes], preferred_element_type=F32)
        second_half = lanes.start >= blk
        if (kind == "diag0" and second_half) or (kind == "diag1" and not second_half):
            within = slice(lanes.start % blk, lanes.start % blk + QUERY_CHUNK)
            return jnp.where(causal[:, within], st, MASKED)
        assert kind != "diag0"
        return st

    all_chunks, hi_chunks = _query_chunks(blk, 1)
    _attn_pipeline(qi, scores, all_chunks, hi_chunks, vt_ref, *scratch)
    acc = scratch[-1][...]
    ot = acc[:HEAD_DIM] / acc[HEAD_DIM:HEAD_DIM + 1]
    o_ref[0] = (ot.T * gate_ref[0]).astype(o_ref.dtype)


def _fox_attn(q, c_rows, k, v, gate, blk):
    b, s, _ = v.shape
    qw = Q_PER_K * blk
    return pl.pallas_call(
        functools.partial(_fox_attn_kernel, blk=blk),
        grid=(b, N_HEADS, s // qw),
        in_specs=[
            pl.BlockSpec((1, qw, HEAD_DIM), lambda bi, h, qi: (bi, qi, h)),
            pl.BlockSpec((1, 1, 1, qw), lambda bi, h, qi: (bi, h, 0, qi)),
            pl.BlockSpec((1, s, AUG), lambda bi, h, qi: (bi, 0, h)),
            pl.BlockSpec((1, s, HEAD_DIM), lambda bi, h, qi: (bi, 0, h)),
            pl.BlockSpec((1, qw, HEAD_DIM), lambda bi, h, qi: (bi, qi, h)),
        ],
        out_specs=pl.BlockSpec((1, qw, HEAD_DIM), lambda bi, h, qi: (bi, qi, h)),
        out_shape=jax.ShapeDtypeStruct((b, s, GROUP_WIDTH), BF16),
        scratch_shapes=_attn_scratch(s // blk, blk, qw),
        compiler_params=_cparams(("arbitrary", "arbitrary", "arbitrary")),
        name="fox_attn",
    )(q, c_rows, k, v, gate)


def _diff_attn_kernel(q_ref, k_ref, v_ref, gate_ref, table_ref, lam_ref, onorm_ref, o_ref, bias_ref, vt_ref,
                      *scratch, blk, lambda_init):
    qi = pl.program_id(2)

    @pl.when(qi == 0)
    def _():
        _fill_vt(v_ref, vt_ref, blk)
        shifted = pltpu.roll(jnp.broadcast_to(table_ref[0], (blk, 2 * blk)), 0, 1, stride=1, stride_axis=0)
        key = lax.broadcasted_iota(jnp.int32, (blk, blk), 0)
        qry = lax.broadcasted_iota(jnp.int32, (blk, blk), 1)
        bias_ref[0] = jnp.where(key <= qry, shifted[:, :blk], MASKED)
        bias_ref[1] = shifted[:, blk:]

    qw = Q_PER_K * blk
    qt = q_ref[0].astype(F32).T
    dim = lax.broadcasted_iota(jnp.int32, qt.shape, 0)
    qzt = jnp.concatenate([jnp.where(dim < DIFF_QK, qt, 0.0), jnp.where(dim >= DIFF_QK, qt, 0.0)],
                          axis=1).astype(BF16)
    def scores(kb, kind, lanes):
        k = k_ref[0, pl.ds(pl.multiple_of(kb * blk, blk), blk), :]
        st = jnp.dot(k, qzt[:, lanes], preferred_element_type=F32)
        second_half = lanes.start % qw >= blk
        within = slice(lanes.start % blk, lanes.start % blk + QUERY_CHUNK)
        if kind == "diag0":
            assert second_half
            return st + bias_ref[0, :, within]
        if kind == "diag1":
            return st + bias_ref[1 if second_half else 0, :, within]
        if kind == "near" and not second_half:
            return st + bias_ref[1, :, within]
        return st

    all_chunks, hi_chunks = _query_chunks(blk, 2)
    _attn_pipeline(qi, scores, all_chunks, hi_chunks, vt_ref, *scratch)

    lam_p = lam_ref[...]
    lam = (jnp.exp(jnp.sum(lam_p[0:1] * lam_p[1:2], axis=-1, keepdims=True))
           - jnp.exp(jnp.sum(lam_p[2:3] * lam_p[3:4], axis=-1, keepdims=True)) + lambda_init)
    acc = scratch[-1][...]
    ot = acc[:HEAD_DIM] / acc[HEAD_DIM:HEAD_DIM + 1]
    o = (ot[:, :qw] - lam * ot[:, qw:]).T
    ms = jnp.mean(o * o, axis=-1, keepdims=True)
    o = o * lax.rsqrt(ms + EPS) * onorm_ref[...] * (1.0 - lambda_init)
    o_ref[0] = (o * gate_ref[0]).astype(o_ref.dtype)


def _diff_attn(q, k, v, gate, bias_table, lam_params, onorm, blk, lambda_init):
    b, s, _ = v.shape
    qw = Q_PER_K * blk
    return pl.pallas_call(
        functools.partial(_diff_attn_kernel, blk=blk, lambda_init=lambda_init),
        grid=(b, N_HEADS, s // qw),
        in_specs=[
            pl.BlockSpec((1, qw, HEAD_DIM), lambda bi, h, qi: (bi, qi, h)),
            pl.BlockSpec((1, s, HEAD_DIM), lambda bi, h, qi: (bi, 0, h)),
            pl.BlockSpec((1, s, HEAD_DIM), lambda bi, h, qi: (bi, 0, h)),
            pl.BlockSpec((1, qw, HEAD_DIM), lambda bi, h, qi: (bi, qi, h)),
            pl.BlockSpec((1, 1, 2 * blk), lambda bi, h, qi: (h, 0, 0)),
            pl.BlockSpec((4, DIFF_QK), lambda bi, h, qi: (0, 0)),
            pl.BlockSpec((1, HEAD_DIM), lambda bi, h, qi: (0, 0)),
        ],
        out_specs=pl.BlockSpec((1, qw, HEAD_DIM), lambda bi, h, qi: (bi, qi, h)),
        out_shape=jax.ShapeDtypeStruct((b, s, GROUP_WIDTH), BF16),
        scratch_shapes=[pltpu.VMEM((2, blk, blk), F32)] + _attn_scratch(s // blk, blk, 2 * qw),
        compiler_params=_cparams(("arbitrary", "arbitrary", "arbitrary")),
        name="diff_attn",
    )(q, k, v, gate, bias_table, lam_params, onorm)


def _out_ple_kernel(x_ref, ma_ref, mb_ref, p_ref, wo_ref, wp_ref, gn_ref, wg_ref, o_ref):
    half = ma_ref.shape[1]
    h = x_ref[...]
    h = h + jnp.dot(ma_ref[...], wo_ref[:half, :], preferred_element_type=F32)
    h = h + jnp.dot(mb_ref[...], wo_ref[half:, :], preferred_element_type=F32)
    ms = jnp.mean(h * h, axis=-1, keepdims=True)
    hn = (h * lax.rsqrt(ms + EPS) * gn_ref[...]).astype(BF16)
    zg = jnp.dot(hn, wg_ref[...], preferred_element_type=F32)
    g = 1.0 / (1.0 + jnp.exp(-zg))
    e = jnp.dot(p_ref[...].astype(BF16), wp_ref[...], preferred_element_type=F32)
    o_ref[...] = h + g * e


def _out_ple(x2, mix_a, mix_b, p2, w_out, ple_proj, gate_norm, ple_gate, tm):
    m, d = x2.shape
    half = mix_a.shape[1]
    pdim = p2.shape[1]
    const = lambda i: (0, 0)
    single = dict(pipeline_mode=pl.Buffered(1))
    return pl.pallas_call(
        _out_ple_kernel,
        grid=(m // tm,),
        in_specs=[
            pl.BlockSpec((tm, d), lambda i: (i, 0)),
            pl.BlockSpec((tm, half), lambda i: (i, 0)),
            pl.BlockSpec((tm, half), lambda i: (i, 0)),
            pl.BlockSpec((tm, pdim), lambda i: (i, 0)),
            pl.BlockSpec((2 * half, d), const, **single),
            pl.BlockSpec((pdim, d), const, **single),
            pl.BlockSpec((1, d), const),
            pl.BlockSpec((d, d), const, **single),
        ],
        out_specs=pl.BlockSpec((tm, d), lambda i: (i, 0)),
        out_shape=jax.ShapeDtypeStruct((m, d), F32),
        compiler_params=_cparams(("arbitrary",)),
        name="out_ple",
    )(x2, mix_a, mix_b, p2, w_out, ple_proj, gate_norm, ple_gate)


def _t5_bucket(n):
    max_exact = N_BUCKETS // 2
    nf = jnp.maximum(n, 1).astype(F32)
    large = max_exact + (jnp.log(nf / max_exact) / math.log(MAX_DISTANCE / max_exact)
                         * (N_BUCKETS - max_exact)).astype(jnp.int32)
    large = jnp.minimum(large, N_BUCKETS - 1)
    return jnp.where(n < max_exact, n, large)


def _bias_table(rel_bias, blk):
    assert blk >= MAX_DISTANCE
    table = rel_bias.astype(F32)[_t5_bucket(jnp.arange(2 * blk))].T
    table = (table - rel_bias.astype(F32)[N_BUCKETS - 1][:, None]) * LOG2E
    return table[:, None, :]


def kernel(x, p, attn_norm, w_in, b_forget, fox_q_norm, fox_k_norm, diff_q_norm, diff_k_norm,
           lambda_q1, lambda_k1, lambda_q2, lambda_k2, diff_out_norm, w_out, rel_bias,
           ple_proj, ple_gate_norm, ple_gate):
    b, s, d = x.shape
    depth = w_in.shape[0]
    gw = GROUP_WIDTH
    blk = 512
    tm = 1024
    bias_table = _bias_table(rel_bias, blk)

    h = x.reshape(b * s, d)
    for i in range(depth):
        lambda_init = 0.8 - 0.6 * math.exp(-0.3 * i)
        w = w_in[i].astype(F32)
        w_f = jnp.pad(w[:, 8 * gw:].astype(BF16), ((0, 0), (0, LANES - N_HEADS)))
        b_f = jnp.pad(b_forget[i].astype(F32), (0, LANES - N_HEADS)).reshape(1, LANES)
        u, c2 = _norm_forget(h, attn_norm[i].astype(F32).reshape(1, d), w_f, b_f, s, 512)

        row = lambda a: a.astype(F32).reshape(1, -1)
        vec_spec = lambda n: pl.BlockSpec((1, n), lambda r: (0, 0))
        c_spec = pl.BlockSpec((tm, LANES), lambda r: (r, 0))
        fox_scale = HEAD_DIM ** -0.5 * LOG2E
        diff_scale = DIFF_QK ** -0.5 * LOG2E

        qa = _proj(functools.partial(_proj_heads_kernel, scale=fox_scale), u, w, 0,
                   [row(fox_q_norm[i])], [vec_spec(HEAD_DIM)], gw, BF16, tm, "proj_fox_q")
        ka = _proj(_proj_fox_key_kernel, u, w, 1,
                   [row(fox_k_norm[i]), c2], [vec_spec(HEAD_DIM), c_spec], N_HEADS * AUG, BF16, tm, "proj_fox_k")
        va = _proj(_proj_plain_kernel, u, w, 2, [], [], gw, BF16, tm, "proj_fox_v")
        ga = _proj(_proj_silu_kernel, u, w, 3, [], [], gw, F32, tm, "proj_fox_gate")
        dq_gain = row(jnp.tile(diff_q_norm[i], 2))
        dk_gain = row(jnp.tile(diff_k_norm[i], 2))
        qb = _proj(functools.partial(_proj_diff_kernel, scale=diff_scale), u, w, 4,
                   [dq_gain], [vec_spec(HEAD_DIM)], gw, BF16, tm, "proj_diff_q")
        kb = _proj(functools.partial(_proj_diff_kernel, scale=1.0), u, w, 5,
                   [dk_gain], [vec_spec(HEAD_DIM)], gw, BF16, tm, "proj_diff_k")
        vb = _proj(_proj_plain_kernel, u, w, 6, [], [], gw, BF16, tm, "proj_diff_v")
        gb = _proj(_proj_silu_kernel, u, w, 7, [], [], gw, F32, tm, "proj_diff_gate")

        to3 = lambda a: a.reshape(b, s, a.shape[-1])
        c_rows = c2[:, :N_HEADS].reshape(b, s, N_HEADS).transpose(0, 2, 1).reshape(b, N_HEADS, 1, s)
        mix_a = _fox_attn(to3(qa), c_rows, to3(ka), to3(va), to3(ga), blk)
        lam_params = jnp.stack([lambda_q1[i], lambda_k1[i], lambda_q2[i], lambda_k2[i]]).astype(F32)
        mix_b = _diff_attn(to3(qb), to3(kb), to3(vb), to3(gb), bias_table, lam_params,
                           row(diff_out_norm[i]), blk, lambda_init)

        h = _out_ple(h, mix_a.reshape(b * s, gw), mix_b.reshape(b * s, gw),
                     p[i].reshape(b * s, -1), w_out[i].astype(BF16), ple_proj[i].astype(BF16),
                     row(ple_gate_norm[i]), ple_gate[i].astype(BF16), 256)
    return h.reshape(b, s, d)
```

```python
import functools
import math

import jax
import jax.numpy as jnp
from jax import lax
from jax.experimental import pallas as pl
from jax.experimental.pallas import tpu as pltpu

F32 = jnp.float32
BF16 = jnp.bfloat16

HEAD_DIM = 128
N_HEADS = 8
DIFF_QK = HEAD_DIM // 2
GROUP_WIDTH = N_HEADS * HEAD_DIM
N_BUCKETS = 32
MAX_DISTANCE = 128
EPS = 1e-6
LOG2E = math.log2(math.e)
MASKED = -3e38
M_FLOOR = -(2.0 ** 100)
Q_PER_K = 2
QUERY_CHUNK = 256
PROJ_CHUNK = 2 * HEAD_DIM
LANES = 128
AUG = 2 * HEAD_DIM
V_ROWS = HEAD_DIM + 16
VMEM_LIMIT = 56 * 1024 * 1024


def _cparams(sem):
    return pltpu.CompilerParams(dimension_semantics=sem, vmem_limit_bytes=VMEM_LIMIT)


def _split3(v):
    p0 = v.astype(BF16).astype(F32)
    r1 = v - p0
    p1 = r1.astype(BF16).astype(F32)
    p2 = (r1 - p1).astype(BF16).astype(F32)
    return p0, p1, p2


def _norm_forget_kernel(x_ref, g_ref, wf_ref, bf_ref, u_ref, c_ref, carry_ref, *, tiles_per_seq):
    i = pl.program_id(0)
    x = x_ref[...]
    tm = x.shape[0]
    ms = jnp.mean(x * x, axis=-1, keepdims=True)
    u = (x * lax.rsqrt(ms + EPS) * g_ref[...]).astype(BF16)
    u_ref[...] = u

    z = jnp.dot(u, wf_ref[...], preferred_element_type=F32) + bf_ref[...]
    lf = (jnp.minimum(z, 0.0) - jnp.log(1.0 + jnp.exp(-jnp.abs(z)))) * LOG2E

    @pl.when(i % tiles_per_seq == 0)
    def _():
        carry_ref[...] = jnp.zeros_like(carry_ref)

    row = lax.broadcasted_iota(jnp.int32, (tm, tm), 0)
    col = lax.broadcasted_iota(jnp.int32, (tm, tm), 1)
    tri = jnp.where(col <= row, 1.0, 0.0).astype(BF16)
    c = carry_ref[...]
    for part in _split3(lf):
        c = c + jnp.dot(tri, part.astype(BF16), preferred_element_type=F32)
    c_ref[...] = c
    carry_ref[...] = c[tm - 1:tm, :]


def _norm_forget(x2, gain, w_f, b_f, seq, tm):
    m, d = x2.shape
    kern = functools.partial(_norm_forget_kernel, tiles_per_seq=seq // tm)
    return pl.pallas_call(
        kern,
        grid=(m // tm,),
        in_specs=[
            pl.BlockSpec((tm, d), lambda i: (i, 0)),
            pl.BlockSpec((1, d), lambda i: (0, 0)),
            pl.BlockSpec((d, LANES), lambda i: (0, 0)),
            pl.BlockSpec((1, LANES), lambda i: (0, 0)),
        ],
        out_specs=[
            pl.BlockSpec((tm, d), lambda i: (i, 0)),
            pl.BlockSpec((tm, LANES), lambda i: (i, 0)),
        ],
        out_shape=[
            jax.ShapeDtypeStruct((m, d), BF16),
            jax.ShapeDtypeStruct((m, LANES), F32),
        ],
        scratch_shapes=[pltpu.VMEM((1, LANES), F32)],
        compiler_params=_cparams(("arbitrary",)),
        name="norm_forget",
    )(x2, gain, w_f, b_f)


def _proj_chunks(u_ref, w_ref, wb_ref):
    @pl.when(pl.program_id(0) == 0)
    def _():
        wb_ref[...] = w_ref[...].astype(BF16)

    u = u_ref[...]
    for c in range(wb_ref.shape[1] // PROJ_CHUNK):
        cols = slice(c * PROJ_CHUNK, (c + 1) * PROJ_CHUNK)
        yield c * (PROJ_CHUNK // HEAD_DIM), jnp.dot(u, wb_ref[:, cols], preferred_element_type=F32)


def _heads_of(h0, y):
    for j in range(PROJ_CHUNK // HEAD_DIM):
        yield h0 + j, y[:, j * HEAD_DIM:(j + 1) * HEAD_DIM]


def _proj_vt_kernel(u_ref, w_ref, o_ref, wb_ref, *, blk):
    tail_row = lax.broadcasted_iota(jnp.int32, (V_ROWS - HEAD_DIM, blk), 0)
    tail = jnp.where(tail_row == 0, 1.0, 0.0).astype(BF16)
    for h0, y in _proj_chunks(u_ref, w_ref, wb_ref):
        for h, yh in _heads_of(h0, y):
            yt = yh.T.astype(BF16)
            for kb in range(yt.shape[1] // blk):
                o_ref[0, h, kb, :HEAD_DIM, :] = yt[:, kb * blk:(kb + 1) * blk]
                o_ref[0, h, kb, HEAD_DIM:, :] = tail


def _proj_silu_kernel(u_ref, w_ref, o_ref, wb_ref):
    for h0, y in _proj_chunks(u_ref, w_ref, wb_ref):
        o_ref[:, h0 * HEAD_DIM:h0 * HEAD_DIM + PROJ_CHUNK] = (y * (1.0 / (1.0 + jnp.exp(-y)))).astype(o_ref.dtype)


def _proj_heads_kernel(u_ref, w_ref, g_ref, o_ref, wb_ref, *, scale):
    gain = g_ref[...] * scale
    for h0, y in _proj_chunks(u_ref, w_ref, wb_ref):
        for h, yh in _heads_of(h0, y):
            ms = jnp.mean(yh * yh, axis=-1, keepdims=True)
            o_ref[0, h] = (yh * lax.rsqrt(ms + EPS) * gain).T.astype(BF16)


def _proj_fox_key_kernel(u_ref, w_ref, g_ref, c_ref, o_ref, wb_ref):
    tm = u_ref.shape[0]
    lane = lax.broadcasted_iota(jnp.int32, (tm, LANES), 1)
    gain = g_ref[...]
    c_all = c_ref[...]
    for h0, y in _proj_chunks(u_ref, w_ref, wb_ref):
        for h, yh in _heads_of(h0, y):
            ms = jnp.mean(yh * yh, axis=-1, keepdims=True)
            o_ref[:, h * AUG:h * AUG + HEAD_DIM] = (yh * lax.rsqrt(ms + EPS) * gain).astype(BF16)
            p0, p1, p2 = _split3(-jnp.broadcast_to(c_all[:, h:h + 1], (tm, LANES)))
            aug = jnp.where(lane == 3, p0, jnp.where(lane == 4, p1, jnp.where(lane == 5, p2, 0.0)))
            aug = jnp.where(lane < 3, 1.0, aug)
            o_ref[:, h * AUG + HEAD_DIM:(h + 1) * AUG] = aug.astype(BF16)


def _proj_diff_kernel(u_ref, w_ref, g_ref, o_ref, wb_ref, *, scale, transposed):
    tm = u_ref.shape[0]
    lane = lax.broadcasted_iota(jnp.int32, (tm, LANES), 1)
    lo = lane < DIFF_QK
    gain = g_ref[...] * scale
    for h0, y in _proj_chunks(u_ref, w_ref, wb_ref):
        for h, yh in _heads_of(h0, y):
            sq = yh * yh
            s_lo = jnp.sum(jnp.where(lo, sq, 0.0), axis=-1, keepdims=True)
            s_all = jnp.sum(sq, axis=-1, keepdims=True)
            ms = jnp.where(lo, s_lo, s_all - s_lo) * (1.0 / DIFF_QK)
            normed = yh * lax.rsqrt(ms + EPS) * gain
            if transposed:
                o_ref[0, h] = normed.T.astype(BF16)
            else:
                o_ref[:, h * HEAD_DIM:(h + 1) * HEAD_DIM] = normed.astype(BF16)


def _proj(kern, u, w_in, layer, section, extra, extra_specs, out_spec, out_shape, tm, name):
    m, d = u.shape
    n = GROUP_WIDTH
    return pl.pallas_call(
        kern,
        grid=(m // tm,),
        in_specs=[
            pl.BlockSpec((tm, d), lambda i: (i, 0)),
            pl.BlockSpec((None, d, n), lambda i: (layer, 0, section), pipeline_mode=pl.Buffered(1)),
        ] + extra_specs,
        out_specs=out_spec,
        out_shape=out_shape,
        scratch_shapes=[pltpu.VMEM((d, n), BF16)],
        compiler_params=_cparams(("arbitrary",)),
        name=name,
    )(u, w_in, *extra)


def _attn_scratch(blk, width):
    return [
        pltpu.VMEM((blk, width), BF16),
        pltpu.VMEM((1, width), F32),
        pltpu.VMEM((2, blk, width), BF16),
        pltpu.VMEM((2, 1, width), F32),
        pltpu.VMEM((1, width), F32),
        pltpu.VMEM((V_ROWS, width), F32),
    ]


def _query_chunks(blk, n_maps):
    qw = Q_PER_K * blk
    chunks = [slice(lo, lo + QUERY_CHUNK) for lo in range(0, n_maps * qw, QUERY_CHUNK)]
    return chunks, [c for c in chunks if c.start % qw >= blk]


def _attn_pipeline(qi, scores, vt_of, all_chunks, hi_chunks, s_ref, mx_ref, p_ref, al_ref, m_ref, acc_ref):
    n = Q_PER_K * (qi + 1)

    def key_block(t):
        return jnp.maximum(n - 1 - t, 0)

    def qk(t, kind, lanes):
        st = scores(key_block(t), kind, lanes)
        s_ref[:, lanes] = st.astype(BF16)
        mx_ref[:, lanes] = jnp.max(st, axis=0, keepdims=True)

    def softmax(slot, lanes):
        m_old = m_ref[:, lanes]
        m_new = jnp.maximum(m_old, mx_ref[:, lanes]).astype(BF16)
        m_ref[:, lanes] = m_new.astype(F32)
        al_ref[slot, :, lanes] = jnp.exp2(m_old - m_new.astype(F32))
        p_ref[slot, :, lanes] = jnp.exp2(s_ref[:, lanes] - m_new)

    def pv(t, slot, lanes):
        acc_ref[:, lanes] = al_ref[slot, :, lanes] * acc_ref[:, lanes] + jnp.dot(
            vt_of(key_block(t)), p_ref[slot, :, lanes], preferred_element_type=F32)

    def step(t, slot, kind, pv_chunks=all_chunks, with_qk=True):
        for lanes in all_chunks:
            softmax(slot, lanes)
            if lanes in pv_chunks:
                pv(t - 1, 1 - slot, lanes)
            if with_qk:
                qk(t + 1, kind, lanes)

    m_ref[...] = jnp.full(m_ref.shape, M_FLOOR, F32)
    acc_ref[...] = jnp.zeros(acc_ref.shape, F32)

    for lanes in hi_chunks:
        qk(0, "diag0", lanes)
    for lanes in all_chunks:
        if lanes in hi_chunks:
            softmax(0, lanes)
        qk(1, "diag1", lanes)
    step(1, 1, "near", pv_chunks=hi_chunks)

    @pl.loop(0, jnp.maximum(qi - 1, 0))
    def _(u):
        step(2 * u + 2, 0, "far")
        step(2 * u + 3, 1, "far")

    @pl.when(qi >= 1)
    def _():
        step(n - 2, 0, "far")
        step(n - 1, 1, "far", with_qk=False)

    for lanes in all_chunks:
        pv(n - 1, 1, lanes)


def _fox_attn_kernel(q_ref, c_ref, k_ref, vt_ref, gate_ref, o_ref, *scratch, blk):
    qi = pl.program_id(2)

    qw = Q_PER_K * blk
    p0, p1, p2 = _split3(c_ref[0, 0])
    dim = lax.broadcasted_iota(jnp.int32, (AUG - HEAD_DIM, qw), 0)
    decay = jnp.where(dim == 0, p0, jnp.where(dim == 1, p1, jnp.where(dim == 2, p2, 0.0)))
    decay = jnp.where((dim >= 3) & (dim < 6), 1.0, decay)
    qt = jnp.concatenate([q_ref[0, 0], decay.astype(BF16)], axis=0)
    key = lax.broadcasted_iota(jnp.int32, (blk, blk), 0)
    qry = lax.broadcasted_iota(jnp.int32, (blk, blk), 1)
    causal = key <= qry

    def scores(kb, kind, lanes):
        k = k_ref[0, pl.ds(pl.multiple_of(kb * blk, blk), blk), :]
        st = jnp.dot(k, qt[:, lanes], preferred_element_type=F32)
        second_half = lanes.start >= blk
        if (kind == "diag0" and second_half) or (kind == "diag1" and not second_half):
            within = slice(lanes.start % blk, lanes.start % blk + QUERY_CHUNK)
            return jnp.where(causal[:, within], st, MASKED)
        assert kind != "diag0"
        return st

    all_chunks, hi_chunks = _query_chunks(blk, 1)
    _attn_pipeline(qi, scores, lambda kb: vt_ref[0, 0, kb], all_chunks, hi_chunks, *scratch)
    acc = scratch[-1][...]
    ot = acc[:HEAD_DIM] / acc[HEAD_DIM:HEAD_DIM + 1]
    o_ref[0] = (ot.T * gate_ref[0]).astype(o_ref.dtype)


def _fox_attn(qt, c_rows, k, vt, gate, blk):
    b, s, _ = k.shape
    qw = Q_PER_K * blk
    return pl.pallas_call(
        functools.partial(_fox_attn_kernel, blk=blk),
        grid=(b, N_HEADS, s // qw),
        in_specs=[
            pl.BlockSpec((1, 1, HEAD_DIM, qw), lambda bi, h, qi: (bi, h, 0, qi)),
            pl.BlockSpec((1, 1, 1, qw), lambda bi, h, qi: (bi, h, 0, qi)),
            pl.BlockSpec((1, s, AUG), lambda bi, h, qi: (bi, 0, h)),
            pl.BlockSpec((1, 1, s // blk, V_ROWS, blk), lambda bi, h, qi: (bi, h, 0, 0, 0)),
            pl.BlockSpec((1, qw, HEAD_DIM), lambda bi, h, qi: (bi, qi, h)),
        ],
        out_specs=pl.BlockSpec((1, qw, HEAD_DIM), lambda bi, h, qi: (bi, qi, h)),
        out_shape=jax.ShapeDtypeStruct((b, s, GROUP_WIDTH), BF16),
        scratch_shapes=_attn_scratch(blk, qw),
        compiler_params=_cparams(("arbitrary", "arbitrary", "arbitrary")),
        name="fox_attn",
    )(qt, c_rows, k, vt, gate)


def _diff_attn_kernel(q_ref, k_ref, vt_ref, gate_ref, table_ref, lam_ref, onorm_ref, o_ref, bias_ref,
                      *scratch, blk, lambda_init):
    qi = pl.program_id(2)

    @pl.when(qi == 0)
    def _():
        shifted = pltpu.roll(jnp.broadcast_to(table_ref[0], (blk, 2 * blk)), 0, 1, stride=1, stride_axis=0)
        key = lax.broadcasted_iota(jnp.int32, (blk, blk), 0)
        qry = lax.broadcasted_iota(jnp.int32, (blk, blk), 1)
        bias_ref[0] = jnp.where(key <= qry, shifted[:, :blk], MASKED)
        bias_ref[1] = shifted[:, blk:]

    qw = Q_PER_K * blk
    qt = q_ref[0, 0]
    zeros = jnp.zeros((DIFF_QK, qw), BF16)
    qzt = jnp.concatenate([jnp.concatenate([qt[:DIFF_QK], zeros], axis=0),
                           jnp.concatenate([zeros, qt[DIFF_QK:]], axis=0)], axis=1)

    def scores(kb, kind, lanes):
        k = k_ref[0, pl.ds(pl.multiple_of(kb * blk, blk), blk), :]
        st = jnp.dot(k, qzt[:, lanes], preferred_element_type=F32)
        second_half = lanes.start % qw >= blk
        within = slice(lanes.start % blk, lanes.start % blk + QUERY_CHUNK)
        if kind == "diag0":
            assert second_half
            return st + bias_ref[0, :, within]
        if kind == "diag1":
            return st + bias_ref[1 if second_half else 0, :, within]
        if kind == "near" and not second_half:
            return st + bias_ref[1, :, within]
        return st

    all_chunks, hi_chunks = _query_chunks(blk, 2)
    _attn_pipeline(qi, scores, lambda kb: vt_ref[0, 0, kb], all_chunks, hi_chunks, *scratch)

    lam_p = lam_ref[...]
    lam = (jnp.exp(jnp.sum(lam_p[0:1] * lam_p[1:2], axis=-1, keepdims=True))
           - jnp.exp(jnp.sum(lam_p[2:3] * lam_p[3:4], axis=-1, keepdims=True)) + lambda_init)
    acc = scratch[-1][...]
    ot = acc[:HEAD_DIM] / acc[HEAD_DIM:HEAD_DIM + 1]
    o = (ot[:, :qw] - lam * ot[:, qw:]).T
    ms = jnp.mean(o * o, axis=-1, keepdims=True)
    o = o * lax.rsqrt(ms + EPS) * onorm_ref[...] * (1.0 - lambda_init)
    o_ref[0] = (o * gate_ref[0]).astype(o_ref.dtype)


def _diff_attn(qt, k, vt, gate, bias_table, lam_params, onorm, blk, lambda_init):
    b, s, _ = k.shape
    qw = Q_PER_K * blk
    return pl.pallas_call(
        functools.partial(_diff_attn_kernel, blk=blk, lambda_init=lambda_init),
        grid=(b, N_HEADS, s // qw),
        in_specs=[
            pl.BlockSpec((1, 1, HEAD_DIM, qw), lambda bi, h, qi: (bi, h, 0, qi)),
            pl.BlockSpec((1, s, HEAD_DIM), lambda bi, h, qi: (bi, 0, h)),
            pl.BlockSpec((1, 1, s // blk, V_ROWS, blk), lambda bi, h, qi: (bi, h, 0, 0, 0)),
            pl.BlockSpec((1, qw, HEAD_DIM), lambda bi, h, qi: (bi, qi, h)),
            pl.BlockSpec((1, 1, 2 * blk), lambda bi, h, qi: (h, 0, 0)),
            pl.BlockSpec((4, DIFF_QK), lambda bi, h, qi: (0, 0)),
            pl.BlockSpec((1, HEAD_DIM), lambda bi, h, qi: (0, 0)),
        ],
        out_specs=pl.BlockSpec((1, qw, HEAD_DIM), lambda bi, h, qi: (bi, qi, h)),
        out_shape=jax.ShapeDtypeStruct((b, s, GROUP_WIDTH), BF16),
        scratch_shapes=[pltpu.VMEM((2, blk, blk), F32)] + _attn_scratch(blk, 2 * qw),
        compiler_params=_cparams(("arbitrary", "arbitrary", "arbitrary")),
        name="diff_attn",
    )(qt, k, vt, gate, bias_table, lam_params, onorm)


def _out_ple_kernel(x_ref, ma_ref, mb_ref, p_ref, wo_ref, wp_ref, gn_ref, wg_ref, o_ref):
    half = ma_ref.shape[1]
    h = x_ref[...]
    h = h + jnp.dot(ma_ref[...], wo_ref[:half, :], preferred_element_type=F32)
    h = h + jnp.dot(mb_ref[...], wo_ref[half:, :], preferred_element_type=F32)
    ms = jnp.mean(h * h, axis=-1, keepdims=True)
    hn = (h * lax.rsqrt(ms + EPS) * gn_ref[...]).astype(BF16)
    zg = jnp.dot(hn, wg_ref[...], preferred_element_type=F32)
    g = 1.0 / (1.0 + jnp.exp(-zg))
    e = jnp.dot(p_ref[...].astype(BF16), wp_ref[...], preferred_element_type=F32)
    o_ref[...] = h + g * e


def _out_ple(x2, mix_a, mix_b, p2, w_out, ple_proj, gate_norm, ple_gate, tm):
    m, d = x2.shape
    half = mix_a.shape[1]
    pdim = p2.shape[1]
    const = lambda i: (0, 0)
    single = dict(pipeline_mode=pl.Buffered(1))
    return pl.pallas_call(
        _out_ple_kernel,
        grid=(m // tm,),
        in_specs=[
            pl.BlockSpec((tm, d), lambda i: (i, 0)),
            pl.BlockSpec((tm, half), lambda i: (i, 0)),
            pl.BlockSpec((tm, half), lambda i: (i, 0)),
            pl.BlockSpec((tm, pdim), lambda i: (i, 0)),
            pl.BlockSpec((2 * half, d), const, **single),
            pl.BlockSpec((pdim, d), const, **single),
            pl.BlockSpec((1, d), const),
            pl.BlockSpec((d, d), const, **single),
        ],
        out_specs=pl.BlockSpec((tm, d), lambda i: (i, 0)),
        out_shape=jax.ShapeDtypeStruct((m, d), F32),
        compiler_params=_cparams(("arbitrary",)),
        name="out_ple",
    )(x2, mix_a, mix_b, p2, w_out, ple_proj, gate_norm, ple_gate)


def _t5_bucket(n):
    max_exact = N_BUCKETS // 2
    nf = jnp.maximum(n, 1).astype(F32)
    large = max_exact + (jnp.log(nf / max_exact) / math.log(MAX_DISTANCE / max_exact)
                         * (N_BUCKETS - max_exact)).astype(jnp.int32)
    large = jnp.minimum(large, N_BUCKETS - 1)
    return jnp.where(n < max_exact, n, large)


def _bias_table(rel_bias, blk):
    assert blk >= MAX_DISTANCE
    table = rel_bias.astype(F32)[_t5_bucket(jnp.arange(2 * blk))].T
    table = (table - rel_bias.astype(F32)[N_BUCKETS - 1][:, None]) * LOG2E
    return table[:, None, :]


def kernel(x, p, attn_norm, w_in, b_forget, fox_q_norm, fox_k_norm, diff_q_norm, diff_k_norm,
           lambda_q1, lambda_k1, lambda_q2, lambda_k2, diff_out_norm, w_out, rel_bias,
           ple_proj, ple_gate_norm, ple_gate):
    b, s, d = x.shape
    depth = w_in.shape[0]
    gw = GROUP_WIDTH
    blk = 512
    tm = 1024
    bias_table = _bias_table(rel_bias, blk)
    w_in = w_in.astype(F32)

    h = x.reshape(b * s, d)
    for i in range(depth):
        lambda_init = 0.8 - 0.6 * math.exp(-0.3 * i)
        w_f = jnp.pad(w_in[i][:, 8 * gw:].astype(BF16), ((0, 0), (0, LANES - N_HEADS)))
        b_f = jnp.pad(b_forget[i].astype(F32), (0, LANES - N_HEADS)).reshape(1, LANES)
        u, c2 = _norm_forget(h, attn_norm[i].astype(F32).reshape(1, d), w_f, b_f, s, 512)

        row = lambda a: a.astype(F32).reshape(1, -1)
        vec_spec = lambda n: pl.BlockSpec((1, n), lambda r: (0, 0))
        c_spec = pl.BlockSpec((tm, LANES), lambda r: (r, 0))
        fox_scale = HEAD_DIM ** -0.5 * LOG2E
        diff_scale = DIFF_QK ** -0.5 * LOG2E

        tiles = s // tm
        flat = lambda width, dtype: (pl.BlockSpec((tm, width), lambda r: (r, 0)),
                                     jax.ShapeDtypeStruct((b * s, width), dtype))
        heads_t = (pl.BlockSpec((1, N_HEADS, HEAD_DIM, tm), lambda r: (r // tiles, 0, 0, r % tiles)),
                   jax.ShapeDtypeStruct((b, N_HEADS, HEAD_DIM, s), BF16))
        v_t = (pl.BlockSpec((1, N_HEADS, tm // blk, V_ROWS, blk), lambda r: (r // tiles, 0, r % tiles, 0, 0)),
               jax.ShapeDtypeStruct((b, N_HEADS, s // blk, V_ROWS, blk), BF16))
        proj = lambda kern, section, extra, specs, out, name: _proj(
            kern, u, w_in, i, section, extra, specs, out[0], out[1], tm, name)
        vt_kernel = functools.partial(_proj_vt_kernel, blk=blk)

        qa = proj(functools.partial(_proj_heads_kernel, scale=fox_scale), 0,
                  [row(fox_q_norm[i])], [vec_spec(HEAD_DIM)], heads_t, "proj_fox_q")
        ka = proj(_proj_fox_key_kernel, 1, [row(fox_k_norm[i]), c2], [vec_spec(HEAD_DIM), c_spec],
                  flat(N_HEADS * AUG, BF16), "proj_fox_k")
        va = proj(vt_kernel, 2, [], [], v_t, "proj_fox_v")
        ga = proj(_proj_silu_kernel, 3, [], [], flat(gw, F32), "proj_fox_gate")
        dq_gain = row(jnp.tile(diff_q_norm[i], 2))
        dk_gain = row(jnp.tile(diff_k_norm[i], 2))
        qb = proj(functools.partial(_proj_diff_kernel, scale=diff_scale, transposed=True), 4,
                  [dq_gain], [vec_spec(HEAD_DIM)], heads_t, "proj_diff_q")
        kb = proj(functools.partial(_proj_diff_kernel, scale=1.0, transposed=False), 5,
                  [dk_gain], [vec_spec(HEAD_DIM)], flat(gw, BF16), "proj_diff_k")
        vb = proj(vt_kernel, 6, [], [], v_t, "proj_diff_v")
        gb = proj(_proj_silu_kernel, 7, [], [], flat(gw, F32), "proj_diff_gate")

        to3 = lambda a: a.reshape(b, s, a.shape[-1])
        c_rows = c2[:, :N_HEADS].reshape(b, s, N_HEADS).transpose(0, 2, 1).reshape(b, N_HEADS, 1, s)
        mix_a = _fox_attn(qa, c_rows, to3(ka), va, to3(ga), blk)
        lam_params = jnp.stack([lambda_q1[i], lambda_k1[i], lambda_q2[i], lambda_k2[i]]).astype(F32)
        mix_b = _diff_attn(qb, to3(kb), vb, to3(gb), bias_table, lam_params,
                           row(diff_out_norm[i]), blk, lambda_init)

        h = _out_ple(h, mix_a.reshape(b * s, gw), mix_b.reshape(b * s, gw),
                     p[i].reshape(b * s, -1), w_out[i].astype(BF16), ple_proj[i].astype(BF16),
                     row(ple_gate_norm[i]), ple_gate[i].astype(BF16), 256)
    return h.reshape(b, s, d)
```

```python
import functools
import math

import jax
import jax.numpy as jnp
from jax import lax
from jax.experimental import pallas as pl
from jax.experimental.pallas import tpu as pltpu

F32 = jnp.float32
BF16 = jnp.bfloat16

HEAD_DIM = 128
N_HEADS = 8
DIFF_QK = HEAD_DIM // 2
GROUP_WIDTH = N_HEADS * HEAD_DIM
N_BUCKETS = 32
MAX_DISTANCE = 128
EPS = 1e-6
LOG2E = math.log2(math.e)
MASKED = -3e38
M_FLOOR = -(2.0 ** 100)
Q_PER_K = 2
QUERY_CHUNK = 256
PROJ_CHUNK = 2 * HEAD_DIM
LANES = 128
AUG = 2 * HEAD_DIM
V_ROWS = HEAD_DIM + 16
VMEM_LIMIT = 56 * 1024 * 1024


def _cparams(sem):
    return pltpu.CompilerParams(dimension_semantics=sem, vmem_limit_bytes=VMEM_LIMIT)


_NT = (((1,), (1,)), ((), ()))


def _split3(v):
    p0 = v.astype(BF16).astype(F32)
    r1 = v - p0
    p1 = r1.astype(BF16).astype(F32)
    p2 = (r1 - p1).astype(BF16).astype(F32)
    return p0, p1, p2


def _norm_forget_kernel(x_ref, g_ref, wf_ref, bf_ref, u_ref, c_ref, carry_ref, *, tiles_per_seq):
    i = pl.program_id(0)
    x = x_ref[...]
    tm = x.shape[0]
    ms = jnp.mean(x * x, axis=-1, keepdims=True)
    u = (x * lax.rsqrt(ms + EPS) * g_ref[...]).astype(BF16)
    u_ref[...] = u

    z = lax.dot_general(u, wf_ref[...].astype(BF16), _NT, preferred_element_type=F32) + bf_ref[...]
    lf = (jnp.minimum(z, 0.0) - jnp.log(1.0 + jnp.exp(-jnp.abs(z)))) * LOG2E

    @pl.when(i % tiles_per_seq == 0)
    def _():
        carry_ref[...] = jnp.zeros_like(carry_ref)

    row = lax.broadcasted_iota(jnp.int32, (tm, tm), 0)
    col = lax.broadcasted_iota(jnp.int32, (tm, tm), 1)
    tri = jnp.where(col <= row, 1.0, 0.0).astype(BF16)
    c = carry_ref[...]
    for part in _split3(lf):
        c = c + jnp.dot(tri, part.astype(BF16), preferred_element_type=F32)
    c_ref[...] = c
    carry_ref[...] = c[tm - 1:tm, :]


def _norm_forget(x2, gain, w_f, b_f, seq, tm):
    m, d = x2.shape
    kern = functools.partial(_norm_forget_kernel, tiles_per_seq=seq // tm)
    return pl.pallas_call(
        kern,
        grid=(m // tm,),
        in_specs=[
            pl.BlockSpec((tm, d), lambda i: (i, 0)),
            pl.BlockSpec((1, d), lambda i: (0, 0)),
            pl.BlockSpec((LANES, d), lambda i: (0, 0)),
            pl.BlockSpec((1, LANES), lambda i: (0, 0)),
        ],
        out_specs=[
            pl.BlockSpec((tm, d), lambda i: (i, 0)),
            pl.BlockSpec((tm, LANES), lambda i: (i, 0)),
        ],
        out_shape=[
            jax.ShapeDtypeStruct((m, d), BF16),
            jax.ShapeDtypeStruct((m, LANES), F32),
        ],
        scratch_shapes=[pltpu.VMEM((1, LANES), F32)],
        compiler_params=_cparams(("arbitrary",)),
        name="norm_forget",
    )(x2, gain, w_f, b_f)


def _proj_chunks(u_ref, w_ref, wb_ref):
    @pl.when(pl.program_id(0) == 0)
    def _():
        wb_ref[...] = w_ref[...].astype(BF16)

    u = u_ref[...]
    for c in range(wb_ref.shape[0] // PROJ_CHUNK):
        cols = slice(c * PROJ_CHUNK, (c + 1) * PROJ_CHUNK)
        yield c * (PROJ_CHUNK // HEAD_DIM), lax.dot_general(u, wb_ref[cols, :], _NT, preferred_element_type=F32)


def _heads_of(h0, y):
    for j in range(PROJ_CHUNK // HEAD_DIM):
        yield h0 + j, y[:, j * HEAD_DIM:(j + 1) * HEAD_DIM]


def _proj_vt_kernel(u_ref, w_ref, o_ref, wb_ref, *, blk):
    tail_row = lax.broadcasted_iota(jnp.int32, (V_ROWS - HEAD_DIM, blk), 0)
    tail = jnp.where(tail_row == 0, 1.0, 0.0).astype(BF16)
    for h0, y in _proj_chunks(u_ref, w_ref, wb_ref):
        for h, yh in _heads_of(h0, y):
            yt = yh.T.astype(BF16)
            for kb in range(yt.shape[1] // blk):
                o_ref[0, h, kb, :HEAD_DIM, :] = yt[:, kb * blk:(kb + 1) * blk]
                o_ref[0, h, kb, HEAD_DIM:, :] = tail


def _proj_silu_kernel(u_ref, w_ref, o_ref, wb_ref):
    for h0, y in _proj_chunks(u_ref, w_ref, wb_ref):
        o_ref[:, h0 * HEAD_DIM:h0 * HEAD_DIM + PROJ_CHUNK] = (y * (1.0 / (1.0 + jnp.exp(-y)))).astype(o_ref.dtype)


def _proj_heads_kernel(u_ref, w_ref, g_ref, o_ref, wb_ref, *, scale):
    gain = g_ref[...] * scale
    for h0, y in _proj_chunks(u_ref, w_ref, wb_ref):
        for h, yh in _heads_of(h0, y):
            ms = jnp.mean(yh * yh, axis=-1, keepdims=True)
            o_ref[0, h] = (yh * lax.rsqrt(ms + EPS) * gain).T.astype(BF16)


def _proj_fox_key_kernel(u_ref, w_ref, g_ref, c_ref, o_ref, wb_ref):
    tm = u_ref.shape[0]
    lane = lax.broadcasted_iota(jnp.int32, (tm, LANES), 1)
    gain = g_ref[...]
    c_all = c_ref[...]
    for h0, y in _proj_chunks(u_ref, w_ref, wb_ref):
        for h, yh in _heads_of(h0, y):
            ms = jnp.mean(yh * yh, axis=-1, keepdims=True)
            o_ref[:, h * AUG:h * AUG + HEAD_DIM] = (yh * lax.rsqrt(ms + EPS) * gain).astype(BF16)
            p0, p1, p2 = _split3(-jnp.broadcast_to(c_all[:, h:h + 1], (tm, LANES)))
            aug = jnp.where(lane == 3, p0, jnp.where(lane == 4, p1, jnp.where(lane == 5, p2, 0.0)))
            aug = jnp.where(lane < 3, 1.0, aug)
            o_ref[:, h * AUG + HEAD_DIM:(h + 1) * AUG] = aug.astype(BF16)


def _proj_diff_kernel(u_ref, w_ref, g_ref, o_ref, wb_ref, *, scale, transposed):
    tm = u_ref.shape[0]
    lane = lax.broadcasted_iota(jnp.int32, (tm, LANES), 1)
    lo = lane < DIFF_QK
    gain = g_ref[...] * scale
    for h0, y in _proj_chunks(u_ref, w_ref, wb_ref):
        for h, yh in _heads_of(h0, y):
            sq = yh * yh
            s_lo = jnp.sum(jnp.where(lo, sq, 0.0), axis=-1, keepdims=True)
            s_all = jnp.sum(sq, axis=-1, keepdims=True)
            ms = jnp.where(lo, s_lo, s_all - s_lo) * (1.0 / DIFF_QK)
            normed = yh * lax.rsqrt(ms + EPS) * gain
            if transposed:
                o_ref[0, h] = normed.T.astype(BF16)
            else:
                o_ref[:, h * HEAD_DIM:(h + 1) * HEAD_DIM] = normed.astype(BF16)


def _proj(kern, u, w_t, layer, section, extra, extra_specs, out_spec, out_shape, tm, name):
    m, d = u.shape
    n = GROUP_WIDTH
    return pl.pallas_call(
        kern,
        grid=(m // tm,),
        in_specs=[
            pl.BlockSpec((tm, d), lambda i: (i, 0)),
            pl.BlockSpec((None, n, d), lambda i: (layer, section, 0), pipeline_mode=pl.Buffered(1)),
        ] + extra_specs,
        out_specs=out_spec,
        out_shape=out_shape,
        scratch_shapes=[pltpu.VMEM((n, d), BF16)],
        compiler_params=_cparams(("arbitrary",)),
        name=name,
    )(u, w_t, *extra)


def _attn_scratch(blk, width):
    return [
        pltpu.VMEM((blk, width), BF16),
        pltpu.VMEM((1, width), F32),
        pltpu.VMEM((2, blk, width), BF16),
        pltpu.VMEM((2, 1, width), F32),
        pltpu.VMEM((1, width), F32),
        pltpu.VMEM((V_ROWS, width), F32),
    ]


def _query_chunks(blk, n_maps):
    qw = Q_PER_K * blk
    chunks = [slice(lo, lo + QUERY_CHUNK) for lo in range(0, n_maps * qw, QUERY_CHUNK)]
    return chunks, [c for c in chunks if c.start % qw >= blk]


def _attn_pipeline(qi, scores, vt_of, all_chunks, hi_chunks, s_ref, mx_ref, p_ref, al_ref, m_ref, acc_ref):
    n = Q_PER_K * (qi + 1)

    def key_block(t):
        return jnp.maximum(n - 1 - t, 0)

    def qk(t, kind, lanes):
        st = scores(key_block(t), kind, lanes)
        s_ref[:, lanes] = st.astype(BF16)
        mx_ref[:, lanes] = jnp.max(st, axis=0, keepdims=True)

    def softmax(slot, lanes):
        m_old = m_ref[:, lanes]
        m_new = jnp.maximum(m_old, mx_ref[:, lanes]).astype(BF16)
        m_ref[:, lanes] = m_new.astype(F32)
        al_ref[slot, :, lanes] = jnp.exp2(m_old - m_new.astype(F32))
        p_ref[slot, :, lanes] = jnp.exp2(s_ref[:, lanes] - m_new)

    def pv(t, slot, lanes):
        acc_ref[:, lanes] = al_ref[slot, :, lanes] * acc_ref[:, lanes] + jnp.dot(
            vt_of(key_block(t)), p_ref[slot, :, lanes], preferred_element_type=F32)

    def step(t, slot, kind, pv_chunks=all_chunks, with_qk=True):
        for lanes in all_chunks:
            softmax(slot, lanes)
            if lanes in pv_chunks:
                pv(t - 1, 1 - slot, lanes)
            if with_qk:
                qk(t + 1, kind, lanes)

    m_ref[...] = jnp.full(m_ref.shape, M_FLOOR, F32)
    acc_ref[...] = jnp.zeros(acc_ref.shape, F32)

    for lanes in hi_chunks:
        qk(0, "diag0", lanes)
    for lanes in all_chunks:
        if lanes in hi_chunks:
            softmax(0, lanes)
        qk(1, "diag1", lanes)
    step(1, 1, "near", pv_chunks=hi_chunks)

    @pl.loop(0, jnp.maximum(qi - 1, 0))
    def _(u):
        step(2 * u + 2, 0, "far")
        step(2 * u + 3, 1, "far")

    @pl.when(qi >= 1)
    def _():
        step(n - 2, 0, "far")
        step(n - 1, 1, "far", with_qk=False)

    for lanes in all_chunks:
        pv(n - 1, 1, lanes)


def _fox_attn_kernel(q_ref, c_ref, k_ref, vt_ref, gate_ref, o_ref, *scratch, blk):
    qi = pl.program_id(2)

    qw = Q_PER_K * blk
    p0, p1, p2 = _split3(c_ref[0, 0])
    dim = lax.broadcasted_iota(jnp.int32, (AUG - HEAD_DIM, qw), 0)
    decay = jnp.where(dim == 0, p0, jnp.where(dim == 1, p1, jnp.where(dim == 2, p2, 0.0)))
    decay = jnp.where((dim >= 3) & (dim < 6), 1.0, decay)
    qt = jnp.concatenate([q_ref[0, 0], decay.astype(BF16)], axis=0)
    key = lax.broadcasted_iota(jnp.int32, (blk, blk), 0)
    qry = lax.broadcasted_iota(jnp.int32, (blk, blk), 1)
    causal = key <= qry

    def scores(kb, kind, lanes):
        k = k_ref[0, pl.ds(pl.multiple_of(kb * blk, blk), blk), :]
        st = jnp.dot(k, qt[:, lanes], preferred_element_type=F32)
        second_half = lanes.start >= blk
        if (kind == "diag0" and second_half) or (kind == "diag1" and not second_half):
            within = slice(lanes.start % blk, lanes.start % blk + QUERY_CHUNK)
            return jnp.where(causal[:, within], st, MASKED)
        assert kind != "diag0"
        return st

    all_chunks, hi_chunks = _query_chunks(blk, 1)
    _attn_pipeline(qi, scores, lambda kb: vt_ref[0, 0, kb], all_chunks, hi_chunks, *scratch)
    acc = scratch[-1][...]
    ot = acc[:HEAD_DIM] / acc[HEAD_DIM:HEAD_DIM + 1]
    o_ref[0] = (ot.T * gate_ref[0]).astype(o_ref.dtype)


def _fox_attn(qt, c_rows, k, vt, gate, blk):
    b, s, _ = k.shape
    qw = Q_PER_K * blk
    return pl.pallas_call(
        functools.partial(_fox_attn_kernel, blk=blk),
        grid=(b, N_HEADS, s // qw),
        in_specs=[
            pl.BlockSpec((1, 1, HEAD_DIM, qw), lambda bi, h, qi: (bi, h, 0, qi)),
            pl.BlockSpec((1, 1, 1, qw), lambda bi, h, qi: (bi, h, 0, qi)),
            pl.BlockSpec((1, s, AUG), lambda bi, h, qi: (bi, 0, h)),
            pl.BlockSpec((1, 1, s // blk, V_ROWS, blk), lambda bi, h, qi: (bi, h, 0, 0, 0)),
            pl.BlockSpec((1, qw, HEAD_DIM), lambda bi, h, qi: (bi, qi, h)),
        ],
        out_specs=pl.BlockSpec((1, qw, HEAD_DIM), lambda bi, h, qi: (bi, qi, h)),
        out_shape=jax.ShapeDtypeStruct((b, s, GROUP_WIDTH), BF16),
        scratch_shapes=_attn_scratch(blk, qw),
        compiler_params=_cparams(("arbitrary", "arbitrary", "arbitrary")),
        name="fox_attn",
    )(qt, c_rows, k, vt, gate)


def _diff_attn_kernel(q_ref, k_ref, vt_ref, gate_ref, table_ref, lam_ref, onorm_ref, o_ref, bias_ref,
                      *scratch, blk, lambda_init):
    qi = pl.program_id(2)

    @pl.when(qi == 0)
    def _():
        shifted = pltpu.roll(jnp.broadcast_to(table_ref[0], (blk, 2 * blk)), 0, 1, stride=1, stride_axis=0)
        key = lax.broadcasted_iota(jnp.int32, (blk, blk), 0)
        qry = lax.broadcasted_iota(jnp.int32, (blk, blk), 1)
        bias_ref[0] = jnp.where(key <= qry, shifted[:, :blk], MASKED)
        bias_ref[1] = shifted[:, blk:]

    qw = Q_PER_K * blk
    qt = q_ref[0, 0]
    zeros = jnp.zeros((DIFF_QK, qw), BF16)
    qzt = jnp.concatenate([jnp.concatenate([qt[:DIFF_QK], zeros], axis=0),
                           jnp.concatenate([zeros, qt[DIFF_QK:]], axis=0)], axis=1)

    def scores(kb, kind, lanes):
        k = k_ref[0, pl.ds(pl.multiple_of(kb * blk, blk), blk), :]
        st = jnp.dot(k, qzt[:, lanes], preferred_element_type=F32)
        second_half = lanes.start % qw >= blk
        within = slice(lanes.start % blk, lanes.start % blk + QUERY_CHUNK)
        if kind == "diag0":
            assert second_half
            return st + bias_ref[0, :, within]
        if kind == "diag1":
            return st + bias_ref[1 if second_half else 0, :, within]
        if kind == "near" and not second_half:
            return st + bias_ref[1, :, within]
        return st

    all_chunks, hi_chunks = _query_chunks(blk, 2)
    _attn_pipeline(qi, scores, lambda kb: vt_ref[0, 0, kb], all_chunks, hi_chunks, *scratch)

    lam_p = lam_ref[...]
    lam = (jnp.exp(jnp.sum(lam_p[0:1] * lam_p[1:2], axis=-1, keepdims=True))
           - jnp.exp(jnp.sum(lam_p[2:3] * lam_p[3:4], axis=-1, keepdims=True)) + lambda_init)
    acc = scratch[-1][...]
    ot = acc[:HEAD_DIM] / acc[HEAD_DIM:HEAD_DIM + 1]
    o = (ot[:, :qw] - lam * ot[:, qw:]).T
    ms = jnp.mean(o * o, axis=-1, keepdims=True)
    o = o * lax.rsqrt(ms + EPS) * onorm_ref[...] * (1.0 - lambda_init)
    o_ref[0] = (o * gate_ref[0]).astype(o_ref.dtype)


def _diff_attn(qt, k, vt, gate, bias_table, lam_params, onorm, blk, lambda_init):
    b, s, _ = k.shape
    qw = Q_PER_K * blk
    return pl.pallas_call(
        functools.partial(_diff_attn_kernel, blk=blk, lambda_init=lambda_init),
        grid=(b, N_HEADS, s // qw),
        in_specs=[
            pl.BlockSpec((1, 1, HEAD_DIM, qw), lambda bi, h, qi: (bi, h, 0, qi)),
            pl.BlockSpec((1, s, HEAD_DIM), lambda bi, h, qi: (bi, 0, h)),
            pl.BlockSpec((1, 1, s // blk, V_ROWS, blk), lambda bi, h, qi: (bi, h, 0, 0, 0)),
            pl.BlockSpec((1, qw, HEAD_DIM), lambda bi, h, qi: (bi, qi, h)),
            pl.BlockSpec((1, 1, 2 * blk), lambda bi, h, qi: (h, 0, 0)),
            pl.BlockSpec((4, DIFF_QK), lambda bi, h, qi: (0, 0)),
            pl.BlockSpec((1, HEAD_DIM), lambda bi, h, qi: (0, 0)),
        ],
        out_specs=pl.BlockSpec((1, qw, HEAD_DIM), lambda bi, h, qi: (bi, qi, h)),
        out_shape=jax.ShapeDtypeStruct((b, s, GROUP_WIDTH), BF16),
        scratch_shapes=[pltpu.VMEM((2, blk, blk), F32)] + _attn_scratch(blk, 2 * qw),
        compiler_params=_cparams(("arbitrary", "arbitrary", "arbitrary")),
        name="diff_attn",
    )(qt, k, vt, gate, bias_table, lam_params, onorm)


def _out_ple_kernel(x_ref, ma_ref, mb_ref, p_ref, wo_ref, wp_ref, gn_ref, wg_ref, o_ref):
    half = ma_ref.shape[1]
    h = x_ref[...]
    h = h + jnp.dot(ma_ref[...], wo_ref[:half, :], preferred_element_type=F32)
    h = h + jnp.dot(mb_ref[...], wo_ref[half:, :], preferred_element_type=F32)
    ms = jnp.mean(h * h, axis=-1, keepdims=True)
    hn = (h * lax.rsqrt(ms + EPS) * gn_ref[...]).astype(BF16)
    zg = jnp.dot(hn, wg_ref[...], preferred_element_type=F32)
    g = 1.0 / (1.0 + jnp.exp(-zg))
    e = jnp.dot(p_ref[...].astype(BF16), wp_ref[...], preferred_element_type=F32)
    o_ref[...] = h + g * e


def _out_ple(x2, mix_a, mix_b, p2, w_out, ple_proj, gate_norm, ple_gate, tm):
    m, d = x2.shape
    half = mix_a.shape[1]
    pdim = p2.shape[1]
    const = lambda i: (0, 0)
    single = dict(pipeline_mode=pl.Buffered(1))
    return pl.pallas_call(
        _out_ple_kernel,
        grid=(m // tm,),
        in_specs=[
            pl.BlockSpec((tm, d), lambda i: (i, 0)),
            pl.BlockSpec((tm, half), lambda i: (i, 0)),
            pl.BlockSpec((tm, half), lambda i: (i, 0)),
            pl.BlockSpec((tm, pdim), lambda i: (i, 0)),
            pl.BlockSpec((2 * half, d), const, **single),
            pl.BlockSpec((pdim, d), const, **single),
            pl.BlockSpec((1, d), const),
            pl.BlockSpec((d, d), const, **single),
        ],
        out_specs=pl.BlockSpec((tm, d), lambda i: (i, 0)),
        out_shape=jax.ShapeDtypeStruct((m, d), F32),
        compiler_params=_cparams(("arbitrary",)),
        name="out_ple",
    )(x2, mix_a, mix_b, p2, w_out, ple_proj, gate_norm, ple_gate)


def _t5_bucket(n):
    max_exact = N_BUCKETS // 2
    nf = jnp.maximum(n, 1).astype(F32)
    large = max_exact + (jnp.log(nf / max_exact) / math.log(MAX_DISTANCE / max_exact)
                         * (N_BUCKETS - max_exact)).astype(jnp.int32)
    large = jnp.minimum(large, N_BUCKETS - 1)
    return jnp.where(n < max_exact, n, large)


def _bias_table(rel_bias, blk):
    assert blk >= MAX_DISTANCE
    table = rel_bias.astype(F32)[_t5_bucket(jnp.arange(2 * blk))].T
    table = (table - rel_bias.astype(F32)[N_BUCKETS - 1][:, None]) * LOG2E
    return table[:, None, :]


def kernel(x, p, attn_norm, w_in, b_forget, fox_q_norm, fox_k_norm, diff_q_norm, diff_k_norm,
           lambda_q1, lambda_k1, lambda_q2, lambda_k2, diff_out_norm, w_out, rel_bias,
           ple_proj, ple_gate_norm, ple_gate):
    b, s, d = x.shape
    depth = w_in.shape[0]
    gw = GROUP_WIDTH
    blk = 512
    tm = 1024
    bias_table = _bias_table(rel_bias, blk)
    w_t = jnp.swapaxes(w_in.astype(F32), 1, 2)

    h = x.reshape(b * s, d)
    for i in range(depth):
        lambda_init = 0.8 - 0.6 * math.exp(-0.3 * i)
        w_f = jnp.pad(w_t[i, 8 * gw:, :], ((0, LANES - N_HEADS), (0, 0)))
        b_f = jnp.pad(b_forget[i].astype(F32), (0, LANES - N_HEADS)).reshape(1, LANES)
        u, c2 = _norm_forget(h, attn_norm[i].astype(F32).reshape(1, d), w_f, b_f, s, 512)

        row = lambda a: a.astype(F32).reshape(1, -1)
        vec_spec = lambda n: pl.BlockSpec((1, n), lambda r: (0, 0))
        c_spec = pl.BlockSpec((tm, LANES), lambda r: (r, 0))
        fox_scale = HEAD_DIM ** -0.5 * LOG2E
        diff_scale = DIFF_QK ** -0.5 * LOG2E

        tiles = s // tm
        flat = lambda width, dtype: (pl.BlockSpec((tm, width), lambda r: (r, 0)),
                                     jax.ShapeDtypeStruct((b * s, width), dtype))
        heads_t = (pl.BlockSpec((1, N_HEADS, HEAD_DIM, tm), lambda r: (r // tiles, 0, 0, r % tiles)),
                   jax.ShapeDtypeStruct((b, N_HEADS, HEAD_DIM, s), BF16))
        v_t = (pl.BlockSpec((1, N_HEADS, tm // blk, V_ROWS, blk), lambda r: (r // tiles, 0, r % tiles, 0, 0)),
               jax.ShapeDtypeStruct((b, N_HEADS, s // blk, V_ROWS, blk), BF16))
        proj = lambda kern, section, extra, specs, out, name: _proj(
            kern, u, w_t, i, section, extra, specs, out[0], out[1], tm, name)
        vt_kernel = functools.partial(_proj_vt_kernel, blk=blk)

        qa = proj(functools.partial(_proj_heads_kernel, scale=fox_scale), 0,
                  [row(fox_q_norm[i])], [vec_spec(HEAD_DIM)], heads_t, "proj_fox_q")
        ka = proj(_proj_fox_key_kernel, 1, [row(fox_k_norm[i]), c2], [vec_spec(HEAD_DIM), c_spec],
                  flat(N_HEADS * AUG, BF16), "proj_fox_k")
        va = proj(vt_kernel, 2, [], [], v_t, "proj_fox_v")
        ga = proj(_proj_silu_kernel, 3, [], [], flat(gw, F32), "proj_fox_gate")
        dq_gain = row(jnp.tile(diff_q_norm[i], 2))
        dk_gain = row(jnp.tile(diff_k_norm[i], 2))
        qb = proj(functools.partial(_proj_diff_kernel, scale=diff_scale, transposed=True), 4,
                  [dq_gain], [vec_spec(HEAD_DIM)], heads_t, "proj_diff_q")
        kb = proj(functools.partial(_proj_diff_kernel, scale=1.0, transposed=False), 5,
                  [dk_gain], [vec_spec(HEAD_DIM)], flat(gw, BF16), "proj_diff_k")
        vb = proj(vt_kernel, 6, [], [], v_t, "proj_diff_v")
        gb = proj(_proj_silu_kernel, 7, [], [], flat(gw, F32), "proj_diff_gate")

        to3 = lambda a: a.reshape(b, s, a.shape[-1])
        c_rows = c2[:, :N_HEADS].reshape(b, s, N_HEADS).transpose(0, 2, 1).reshape(b, N_HEADS, 1, s)
        mix_a = _fox_attn(qa, c_rows, to3(ka), va, to3(ga), blk)
        lam_params = jnp.stack([lambda_q1[i], lambda_k1[i], lambda_q2[i], lambda_k2[i]]).astype(F32)
        mix_b = _diff_attn(qb, to3(kb), vb, to3(gb), bias_table, lam_params,
                           row(diff_out_norm[i]), blk, lambda_init)

        h = _out_ple(h, mix_a.reshape(b * s, gw), mix_b.reshape(b * s, gw),
                     p[i].reshape(b * s, -1), w_out[i].astype(BF16), ple_proj[i].astype(BF16),
                     row(ple_gate_norm[i]), ple_gate[i].astype(BF16), 256)
    return h.reshape(b, s, d)
```

```python
import functools
import math

import jax
import jax.numpy as jnp
from jax import lax
from jax.experimental import pallas as pl
from jax.experimental.pallas import tpu as pltpu

F32 = jnp.float32
BF16 = jnp.bfloat16

HEAD_DIM = 128
N_HEADS = 8
DIFF_QK = HEAD_DIM // 2
GROUP_WIDTH = N_HEADS * HEAD_DIM
N_BUCKETS = 32
MAX_DISTANCE = 128
EPS = 1e-6
LOG2E = math.log2(math.e)
MASKED = -3e38
M_FLOOR = -(2.0 ** 100)
Q_PER_K = 2
QUERY_CHUNK = 256
PROJ_CHUNK = 2 * HEAD_DIM
LANES = 128
AUG = 2 * HEAD_DIM
V_ROWS = HEAD_DIM + 16
VMEM_LIMIT = 56 * 1024 * 1024


def _cparams(sem):
    return pltpu.CompilerParams(dimension_semantics=sem, vmem_limit_bytes=VMEM_LIMIT)


_NT = (((1,), (1,)), ((), ()))


def _split3(v):
    p0 = v.astype(BF16).astype(F32)
    r1 = v - p0
    p1 = r1.astype(BF16).astype(F32)
    p2 = (r1 - p1).astype(BF16).astype(F32)
    return p0, p1, p2


def _norm_forget_kernel(x_ref, g_ref, wf_ref, bf_ref, u_ref, c_ref, carry_ref, *, tiles_per_seq):
    i = pl.program_id(0)
    x = x_ref[...]
    tm = x.shape[0]
    ms = jnp.mean(x * x, axis=-1, keepdims=True)
    u = (x * lax.rsqrt(ms + EPS) * g_ref[...]).astype(BF16)
    u_ref[...] = u

    z = lax.dot_general(u, wf_ref[...].astype(BF16), _NT, preferred_element_type=F32) + bf_ref[...]
    lf = (jnp.minimum(z, 0.0) - jnp.log(1.0 + jnp.exp(-jnp.abs(z)))) * LOG2E

    @pl.when(i % tiles_per_seq == 0)
    def _():
        carry_ref[...] = jnp.zeros_like(carry_ref)

    row = lax.broadcasted_iota(jnp.int32, (tm, tm), 0)
    col = lax.broadcasted_iota(jnp.int32, (tm, tm), 1)
    tri = jnp.where(col <= row, 1.0, 0.0).astype(BF16)
    c = carry_ref[...]
    for part in _split3(lf):
        c = c + jnp.dot(tri, part.astype(BF16), preferred_element_type=F32)
    c_ref[...] = c
    carry_ref[...] = c[tm - 1:tm, :]


def _norm_forget(x2, gain, w_f, b_f, seq, tm):
    m, d = x2.shape
    kern = functools.partial(_norm_forget_kernel, tiles_per_seq=seq // tm)
    return pl.pallas_call(
        kern,
        grid=(m // tm,),
        in_specs=[
            pl.BlockSpec((tm, d), lambda i: (i, 0)),
            pl.BlockSpec((1, d), lambda i: (0, 0)),
            pl.BlockSpec((LANES, d), lambda i: (0, 0)),
            pl.BlockSpec((1, LANES), lambda i: (0, 0)),
        ],
        out_specs=[
            pl.BlockSpec((tm, d), lambda i: (i, 0)),
            pl.BlockSpec((tm, LANES), lambda i: (i, 0)),
        ],
        out_shape=[
            jax.ShapeDtypeStruct((m, d), BF16),
            jax.ShapeDtypeStruct((m, LANES), F32),
        ],
        scratch_shapes=[pltpu.VMEM((1, LANES), F32)],
        compiler_params=_cparams(("arbitrary",)),
        name="norm_forget",
    )(x2, gain, w_f, b_f)


def _proj_chunks(u_ref, w_ref, wb_ref):
    @pl.when(pl.program_id(0) == 0)
    def _():
        wb_ref[...] = w_ref[...].astype(BF16)

    u = u_ref[...]
    for c in range(wb_ref.shape[0] // PROJ_CHUNK):
        cols = slice(c * PROJ_CHUNK, (c + 1) * PROJ_CHUNK)
        yield c * (PROJ_CHUNK // HEAD_DIM), lax.dot_general(u, wb_ref[cols, :], _NT, preferred_element_type=F32)


def _heads_of(h0, y):
    for j in range(PROJ_CHUNK // HEAD_DIM):
        yield h0 + j, y[:, j * HEAD_DIM:(j + 1) * HEAD_DIM]


def _proj_vt_kernel(u_ref, w_ref, o_ref, wb_ref, *, blk):
    tail_row = lax.broadcasted_iota(jnp.int32, (V_ROWS - HEAD_DIM, blk), 0)
    tail = jnp.where(tail_row == 0, 1.0, 0.0).astype(BF16)
    for h0, y in _proj_chunks(u_ref, w_ref, wb_ref):
        for h, yh in _heads_of(h0, y):
            yt = yh.T.astype(BF16)
            for kb in range(yt.shape[1] // blk):
                o_ref[0, h, kb, :HEAD_DIM, :] = yt[:, kb * blk:(kb + 1) * blk]
                o_ref[0, h, kb, HEAD_DIM:, :] = tail


def _proj_silu_kernel(u_ref, w_ref, o_ref, wb_ref):
    for h0, y in _proj_chunks(u_ref, w_ref, wb_ref):
        o_ref[:, h0 * HEAD_DIM:h0 * HEAD_DIM + PROJ_CHUNK] = (y * (1.0 / (1.0 + jnp.exp(-y)))).astype(o_ref.dtype)


def _proj_heads_kernel(u_ref, w_ref, g_ref, o_ref, wb_ref, *, scale):
    gain = g_ref[...] * scale
    for h0, y in _proj_chunks(u_ref, w_ref, wb_ref):
        for h, yh in _heads_of(h0, y):
            ms = jnp.mean(yh * yh, axis=-1, keepdims=True)
            o_ref[0, h, 0] = (yh * lax.rsqrt(ms + EPS) * gain).T.astype(BF16)


def _proj_fox_key_kernel(u_ref, w_ref, g_ref, c_ref, o_ref, wb_ref):
    tm = u_ref.shape[0]
    lane = lax.broadcasted_iota(jnp.int32, (tm, LANES), 1)
    gain = g_ref[...]
    c_all = c_ref[...]
    for h0, y in _proj_chunks(u_ref, w_ref, wb_ref):
        for h, yh in _heads_of(h0, y):
            ms = jnp.mean(yh * yh, axis=-1, keepdims=True)
            o_ref[:, h * AUG:h * AUG + HEAD_DIM] = (yh * lax.rsqrt(ms + EPS) * gain).astype(BF16)
            p0, p1, p2 = _split3(-jnp.broadcast_to(c_all[:, h:h + 1], (tm, LANES)))
            aug = jnp.where(lane == 3, p0, jnp.where(lane == 4, p1, jnp.where(lane == 5, p2, 0.0)))
            aug = jnp.where(lane < 3, 1.0, aug)
            o_ref[:, h * AUG + HEAD_DIM:(h + 1) * AUG] = aug.astype(BF16)


def _proj_diff_kernel(u_ref, w_ref, g_ref, o_ref, wb_ref, *, scale, transposed):
    tm = u_ref.shape[0]
    lane = lax.broadcasted_iota(jnp.int32, (tm, LANES), 1)
    lo = lane < DIFF_QK
    gain = g_ref[...] * scale
    for h0, y in _proj_chunks(u_ref, w_ref, wb_ref):
        for h, yh in _heads_of(h0, y):
            sq = yh * yh
            s_lo = jnp.sum(jnp.where(lo, sq, 0.0), axis=-1, keepdims=True)
            s_all = jnp.sum(sq, axis=-1, keepdims=True)
            ms = jnp.where(lo, s_lo, s_all - s_lo) * (1.0 / DIFF_QK)
            normed = yh * lax.rsqrt(ms + EPS) * gain
            if transposed:
                o_ref[0, h, 0] = normed.T.astype(BF16)
            else:
                o_ref[:, h * HEAD_DIM:(h + 1) * HEAD_DIM] = normed.astype(BF16)


def _proj(kern, u, w_t, layer, section, extra, extra_specs, out_spec, out_shape, tm, name):
    m, d = u.shape
    n = GROUP_WIDTH
    return pl.pallas_call(
        kern,
        grid=(m // tm,),
        in_specs=[
            pl.BlockSpec((tm, d), lambda i: (i, 0)),
            pl.BlockSpec((None, n, d), lambda i: (layer, section, 0), pipeline_mode=pl.Buffered(1)),
        ] + extra_specs,
        out_specs=out_spec,
        out_shape=out_shape,
        scratch_shapes=[pltpu.VMEM((n, d), BF16)],
        compiler_params=_cparams(("arbitrary",)),
        name=name,
    )(u, w_t, *extra)


def _attn_scratch(blk, width):
    return [
        pltpu.VMEM((blk, width), BF16),
        pltpu.VMEM((1, width), F32),
        pltpu.VMEM((2, blk, width), BF16),
        pltpu.VMEM((2, 1, width), F32),
        pltpu.VMEM((1, width), F32),
        pltpu.VMEM((V_ROWS, width), F32),
    ]


def _query_chunks(blk, n_maps):
    qw = Q_PER_K * blk
    chunks = [slice(lo, lo + QUERY_CHUNK) for lo in range(0, n_maps * qw, QUERY_CHUNK)]
    return chunks, [c for c in chunks if c.start % qw >= blk]


def _attn_pipeline(qi, scores, vt_of, all_chunks, hi_chunks, s_ref, mx_ref, p_ref, al_ref, m_ref, acc_ref):
    n = Q_PER_K * (qi + 1)

    def key_block(t):
        return jnp.maximum(n - 1 - t, 0)

    def qk(t, kind, lanes):
        st = scores(key_block(t), kind, lanes)
        s_ref[:, lanes] = st.astype(BF16)
        mx_ref[:, lanes] = jnp.max(st, axis=0, keepdims=True)

    def softmax(slot, lanes):
        m_old = m_ref[:, lanes]
        m_new = jnp.maximum(m_old, mx_ref[:, lanes]).astype(BF16)
        m_ref[:, lanes] = m_new.astype(F32)
        al_ref[slot, :, lanes] = jnp.exp2(m_old - m_new.astype(F32))
        p_ref[slot, :, lanes] = jnp.exp2(s_ref[:, lanes] - m_new)

    def pv(t, slot, lanes):
        acc_ref[:, lanes] = al_ref[slot, :, lanes] * acc_ref[:, lanes] + jnp.dot(
            vt_of(key_block(t)), p_ref[slot, :, lanes], preferred_element_type=F32)

    def step(t, slot, kind, pv_chunks=all_chunks, with_qk=True):
        for lanes in all_chunks:
            softmax(slot, lanes)
            if lanes in pv_chunks:
                pv(t - 1, 1 - slot, lanes)
            if with_qk:
                qk(t + 1, kind, lanes)

    m_ref[...] = jnp.full(m_ref.shape, M_FLOOR, F32)
    acc_ref[...] = jnp.zeros(acc_ref.shape, F32)

    for lanes in hi_chunks:
        qk(0, "diag0", lanes)
    for lanes in all_chunks:
        if lanes in hi_chunks:
            softmax(0, lanes)
        qk(1, "diag1", lanes)
    step(1, 1, "near", pv_chunks=hi_chunks)

    @pl.loop(0, jnp.maximum(qi - 1, 0))
    def _(u):
        step(2 * u + 2, 0, "far")
        step(2 * u + 3, 1, "far")

    @pl.when(qi >= 1)
    def _():
        step(n - 2, 0, "far")
        step(n - 1, 1, "far", with_qk=False)

    for lanes in all_chunks:
        pv(n - 1, 1, lanes)


def _fox_attn_kernel(q_ref, c_ref, k_ref, vt_ref, gate_ref, o_ref, *scratch, blk):
    qw = Q_PER_K * blk
    key = lax.broadcasted_iota(jnp.int32, (blk, blk), 0)
    qry = lax.broadcasted_iota(jnp.int32, (blk, blk), 1)
    causal = key <= qry
    dim = lax.broadcasted_iota(jnp.int32, (AUG - HEAD_DIM, qw), 0)
    all_chunks, hi_chunks = _query_chunks(blk, 1)

    @pl.loop(0, q_ref.shape[2])
    def _(qi):
        p0, p1, p2 = _split3(c_ref[0, 0, qi])
        decay = jnp.where(dim == 0, p0, jnp.where(dim == 1, p1, jnp.where(dim == 2, p2, 0.0)))
        decay = jnp.where((dim >= 3) & (dim < 6), 1.0, decay)
        qt = jnp.concatenate([q_ref[0, 0, qi], decay.astype(BF16)], axis=0)

        def scores(kb, kind, lanes):
            k = k_ref[0, pl.ds(pl.multiple_of(kb * blk, blk), blk), :]
            st = jnp.dot(k, qt[:, lanes], preferred_element_type=F32)
            second_half = lanes.start >= blk
            if (kind == "diag0" and second_half) or (kind == "diag1" and not second_half):
                within = slice(lanes.start % blk, lanes.start % blk + QUERY_CHUNK)
                return jnp.where(causal[:, within], st, MASKED)
            assert kind != "diag0"
            return st

        _attn_pipeline(qi, scores, lambda kb: vt_ref[0, 0, kb], all_chunks, hi_chunks, *scratch)
        acc = scratch[-1][...]
        ot = acc[:HEAD_DIM] / acc[HEAD_DIM:HEAD_DIM + 1]
        rows = pl.ds(pl.multiple_of(qi * qw, qw), qw)
        o_ref[0, rows, :] = (ot.T * gate_ref[0, rows, :]).astype(o_ref.dtype)


def _fox_attn(qt, c_rows, k, vt, gate, blk):
    b, s, _ = k.shape
    qw = Q_PER_K * blk
    head = lambda *block: pl.BlockSpec((1, 1) + block, lambda bi, h: (bi, h) + (0,) * len(block))
    cols = lambda width: pl.BlockSpec((1, s, width), lambda bi, h: (bi, 0, h))
    return pl.pallas_call(
        functools.partial(_fox_attn_kernel, blk=blk),
        grid=(b, N_HEADS),
        in_specs=[
            head(s // qw, HEAD_DIM, qw),
            head(s // qw, 1, qw),
            cols(AUG),
            head(s // blk, V_ROWS, blk),
            cols(HEAD_DIM),
        ],
        out_specs=cols(HEAD_DIM),
        out_shape=jax.ShapeDtypeStruct((b, s, GROUP_WIDTH), BF16),
        scratch_shapes=_attn_scratch(blk, qw),
        compiler_params=_cparams(("arbitrary", "arbitrary")),
        name="fox_attn",
    )(qt, c_rows, k, vt, gate)


def _diff_attn_kernel(q_ref, k_ref, vt_ref, gate_ref, table_ref, lam_ref, onorm_ref, o_ref, bias_ref,
                      *scratch, blk, lambda_init):
    qw = Q_PER_K * blk
    shifted = pltpu.roll(jnp.broadcast_to(table_ref[0], (blk, 2 * blk)), 0, 1, stride=1, stride_axis=0)
    key = lax.broadcasted_iota(jnp.int32, (blk, blk), 0)
    qry = lax.broadcasted_iota(jnp.int32, (blk, blk), 1)
    bias_ref[0] = jnp.where(key <= qry, shifted[:, :blk], MASKED)
    bias_ref[1] = shifted[:, blk:]

    lam_p = lam_ref[...]
    lam = (jnp.exp(jnp.sum(lam_p[0:1] * lam_p[1:2], axis=-1, keepdims=True))
           - jnp.exp(jnp.sum(lam_p[2:3] * lam_p[3:4], axis=-1, keepdims=True)) + lambda_init)
    zeros = jnp.zeros((DIFF_QK, qw), BF16)
    all_chunks, hi_chunks = _query_chunks(blk, 2)

    @pl.loop(0, q_ref.shape[2])
    def _(qi):
        qt = q_ref[0, 0, qi]
        qzt = jnp.concatenate([jnp.concatenate([qt[:DIFF_QK], zeros], axis=0),
                               jnp.concatenate([zeros, qt[DIFF_QK:]], axis=0)], axis=1)

        def scores(kb, kind, lanes):
            k = k_ref[0, pl.ds(pl.multiple_of(kb * blk, blk), blk), :]
            st = jnp.dot(k, qzt[:, lanes], preferred_element_type=F32)
            second_half = lanes.start % qw >= blk
            within = slice(lanes.start % blk, lanes.start % blk + QUERY_CHUNK)
            if kind == "diag0":
                assert second_half
                return st + bias_ref[0, :, within]
            if kind == "diag1":
                return st + bias_ref[1 if second_half else 0, :, within]
            if kind == "near" and not second_half:
                return st + bias_ref[1, :, within]
            return st

        _attn_pipeline(qi, scores, lambda kb: vt_ref[0, 0, kb], all_chunks, hi_chunks, *scratch)

        acc = scratch[-1][...]
        ot = acc[:HEAD_DIM] / acc[HEAD_DIM:HEAD_DIM + 1]
        o = (ot[:, :qw] - lam * ot[:, qw:]).T
        ms = jnp.mean(o * o, axis=-1, keepdims=True)
        o = o * lax.rsqrt(ms + EPS) * onorm_ref[...] * (1.0 - lambda_init)
        rows = pl.ds(pl.multiple_of(qi * qw, qw), qw)
        o_ref[0, rows, :] = (o * gate_ref[0, rows, :]).astype(o_ref.dtype)


def _diff_attn(qt, k, vt, gate, bias_table, lam_params, onorm, blk, lambda_init):
    b, s, _ = k.shape
    qw = Q_PER_K * blk
    head = lambda *block: pl.BlockSpec((1, 1) + block, lambda bi, h: (bi, h) + (0,) * len(block))
    cols = lambda width: pl.BlockSpec((1, s, width), lambda bi, h: (bi, 0, h))
    return pl.pallas_call(
        functools.partial(_diff_attn_kernel, blk=blk, lambda_init=lambda_init),
        grid=(b, N_HEADS),
        in_specs=[
            head(s // qw, HEAD_DIM, qw),
            cols(HEAD_DIM),
            head(s // blk, V_ROWS, blk),
            cols(HEAD_DIM),
            pl.BlockSpec((1, 1, 2 * blk), lambda bi, h: (h, 0, 0)),
            pl.BlockSpec((4, DIFF_QK), lambda bi, h: (0, 0)),
            pl.BlockSpec((1, HEAD_DIM), lambda bi, h: (0, 0)),
        ],
        out_specs=cols(HEAD_DIM),
        out_shape=jax.ShapeDtypeStruct((b, s, GROUP_WIDTH), BF16),
        scratch_shapes=[pltpu.VMEM((2, blk, blk), F32)] + _attn_scratch(blk, 2 * qw),
        compiler_params=_cparams(("arbitrary", "arbitrary")),
        name="diff_attn",
    )(qt, k, vt, gate, bias_table, lam_params, onorm)


def _out_ple_kernel(x_ref, ma_ref, mb_ref, p_ref, wo_ref, wp_ref, gn_ref, wg_ref, o_ref):
    half = ma_ref.shape[1]
    h = x_ref[...]
    h = h + jnp.dot(ma_ref[...], wo_ref[:half, :], preferred_element_type=F32)
    h = h + jnp.dot(mb_ref[...], wo_ref[half:, :], preferred_element_type=F32)
    ms = jnp.mean(h * h, axis=-1, keepdims=True)
    hn = (h * lax.rsqrt(ms + EPS) * gn_ref[...]).astype(BF16)
    zg = jnp.dot(hn, wg_ref[...], preferred_element_type=F32)
    g = 1.0 / (1.0 + jnp.exp(-zg))
    e = jnp.dot(p_ref[...].astype(BF16), wp_ref[...], preferred_element_type=F32)
    o_ref[...] = h + g * e


def _out_ple(x2, mix_a, mix_b, p2, w_out, ple_proj, gate_norm, ple_gate, tm):
    m, d = x2.shape
    half = mix_a.shape[1]
    pdim = p2.shape[1]
    const = lambda i: (0, 0)
    single = dict(pipeline_mode=pl.Buffered(1))
    return pl.pallas_call(
        _out_ple_kernel,
        grid=(m // tm,),
        in_specs=[
            pl.BlockSpec((tm, d), lambda i: (i, 0)),
            pl.BlockSpec((tm, half), lambda i: (i, 0)),
            pl.BlockSpec((tm, half), lambda i: (i, 0)),
            pl.BlockSpec((tm, pdim), lambda i: (i, 0)),
            pl.BlockSpec((2 * half, d), const, **single),
            pl.BlockSpec((pdim, d), const, **single),
            pl.BlockSpec((1, d), const),
            pl.BlockSpec((d, d), const, **single),
        ],
        out_specs=pl.BlockSpec((tm, d), lambda i: (i, 0)),
        out_shape=jax.ShapeDtypeStruct((m, d), F32),
        compiler_params=_cparams(("arbitrary",)),
        name="out_ple",
    )(x2, mix_a, mix_b, p2, w_out, ple_proj, gate_norm, ple_gate)


def _t5_bucket(n):
    max_exact = N_BUCKETS // 2
    nf = jnp.maximum(n, 1).astype(F32)
    large = max_exact + (jnp.log(nf / max_exact) / math.log(MAX_DISTANCE / max_exact)
                         * (N_BUCKETS - max_exact)).astype(jnp.int32)
    large = jnp.minimum(large, N_BUCKETS - 1)
    return jnp.where(n < max_exact, n, large)


def _bias_table(rel_bias, blk):
    assert blk >= MAX_DISTANCE
    table = rel_bias.astype(F32)[_t5_bucket(jnp.arange(2 * blk))].T
    table = (table - rel_bias.astype(F32)[N_BUCKETS - 1][:, None]) * LOG2E
    return table[:, None, :]


def kernel(x, p, attn_norm, w_in, b_forget, fox_q_norm, fox_k_norm, diff_q_norm, diff_k_norm,
           lambda_q1, lambda_k1, lambda_q2, lambda_k2, diff_out_norm, w_out, rel_bias,
           ple_proj, ple_gate_norm, ple_gate):
    b, s, d = x.shape
    depth = w_in.shape[0]
    gw = GROUP_WIDTH
    blk = 512
    tm = 1024
    bias_table = _bias_table(rel_bias, blk)
    w_t = jnp.swapaxes(w_in.astype(F32), 1, 2)

    h = x.reshape(b * s, d)
    for i in range(depth):
        lambda_init = 0.8 - 0.6 * math.exp(-0.3 * i)
        w_f = jnp.pad(w_t[i, 8 * gw:, :], ((0, LANES - N_HEADS), (0, 0)))
        b_f = jnp.pad(b_forget[i].astype(F32), (0, LANES - N_HEADS)).reshape(1, LANES)
        u, c2 = _norm_forget(h, attn_norm[i].astype(F32).reshape(1, d), w_f, b_f, s, 512)

        row = lambda a: a.astype(F32).reshape(1, -1)
        vec_spec = lambda n: pl.BlockSpec((1, n), lambda r: (0, 0))
        c_spec = pl.BlockSpec((tm, LANES), lambda r: (r, 0))
        fox_scale = HEAD_DIM ** -0.5 * LOG2E
        diff_scale = DIFF_QK ** -0.5 * LOG2E

        tiles = s // tm
        flat = lambda width, dtype: (pl.BlockSpec((tm, width), lambda r: (r, 0)),
                                     jax.ShapeDtypeStruct((b * s, width), dtype))
        assert tm == Q_PER_K * blk
        heads_t = (pl.BlockSpec((1, N_HEADS, 1, HEAD_DIM, tm), lambda r: (r // tiles, 0, r % tiles, 0, 0)),
                   jax.ShapeDtypeStruct((b, N_HEADS, tiles, HEAD_DIM, tm), BF16))
        v_t = (pl.BlockSpec((1, N_HEADS, tm // blk, V_ROWS, blk), lambda r: (r // tiles, 0, r % tiles, 0, 0)),
               jax.ShapeDtypeStruct((b, N_HEADS, s // blk, V_ROWS, blk), BF16))
        proj = lambda kern, section, extra, specs, out, name: _proj(
            kern, u, w_t, i, section, extra, specs, out[0], out[1], tm, name)
        vt_kernel = functools.partial(_proj_vt_kernel, blk=blk)

        qa = proj(functools.partial(_proj_heads_kernel, scale=fox_scale), 0,
                  [row(fox_q_norm[i])], [vec_spec(HEAD_DIM)], heads_t, "proj_fox_q")
        ka = proj(_proj_fox_key_kernel, 1, [row(fox_k_norm[i]), c2], [vec_spec(HEAD_DIM), c_spec],
                  flat(N_HEADS * AUG, BF16), "proj_fox_k")
        va = proj(vt_kernel, 2, [], [], v_t, "proj_fox_v")
        ga = proj(_proj_silu_kernel, 3, [], [], flat(gw, F32), "proj_fox_gate")
        dq_gain = row(jnp.tile(diff_q_norm[i], 2))
        dk_gain = row(jnp.tile(diff_k_norm[i], 2))
        qb = proj(functools.partial(_proj_diff_kernel, scale=diff_scale, transposed=True), 4,
                  [dq_gain], [vec_spec(HEAD_DIM)], heads_t, "proj_diff_q")
        kb = proj(functools.partial(_proj_diff_kernel, scale=1.0, transposed=False), 5,
                  [dk_gain], [vec_spec(HEAD_DIM)], flat(gw, BF16), "proj_diff_k")
        vb = proj(vt_kernel, 6, [], [], v_t, "proj_diff_v")
        gb = proj(_proj_silu_kernel, 7, [], [], flat(gw, F32), "proj_diff_gate")

        to3 = lambda a: a.reshape(b, s, a.shape[-1])
        c_rows = c2[:, :N_HEADS].reshape(b, tiles, tm, N_HEADS).transpose(0, 3, 1, 2).reshape(b, N_HEADS, tiles, 1, tm)
        mix_a = _fox_attn(qa, c_rows, to3(ka), va, to3(ga), blk)
        lam_params = jnp.stack([lambda_q1[i], lambda_k1[i], lambda_q2[i], lambda_k2[i]]).astype(F32)
        mix_b = _diff_attn(qb, to3(kb), vb, to3(gb), bias_table, lam_params,
                           row(diff_out_norm[i]), blk, lambda_init)

        h = _out_ple(h, mix_a.reshape(b * s, gw), mix_b.reshape(b * s, gw),
                     p[i].reshape(b * s, -1), w_out[i].astype(BF16), ple_proj[i].astype(BF16),
                     row(ple_gate_norm[i]), ple_gate[i].astype(BF16), 256)
    return h.reshape(b, s, d)
```

```python
import functools
import math

import jax
import jax.numpy as jnp
from jax import lax
from jax.experimental import pallas as pl
from jax.experimental.pallas import tpu as pltpu

F32 = jnp.float32
BF16 = jnp.bfloat16

HEAD_DIM = 128
N_HEADS = 8
DIFF_QK = HEAD_DIM // 2
GROUP_WIDTH = N_HEADS * HEAD_DIM
N_BUCKETS = 32
MAX_DISTANCE = 128
EPS = 1e-6
LOG2E = math.log2(math.e)
MASKED = -3e38
M_FLOOR = -(2.0 ** 100)
Q_PER_K = 4
QUERY_CHUNK = 256
PROJ_CHUNK = 2 * HEAD_DIM
LANES = 128
AUG = 2 * HEAD_DIM
V_ROWS = HEAD_DIM + 16
VMEM_LIMIT = 56 * 1024 * 1024


def _cparams(sem):
    return pltpu.CompilerParams(dimension_semantics=sem, vmem_limit_bytes=VMEM_LIMIT)


_NT = (((1,), (1,)), ((), ()))


def _split3(v):
    p0 = v.astype(BF16).astype(F32)
    r1 = v - p0
    p1 = r1.astype(BF16).astype(F32)
    p2 = (r1 - p1).astype(BF16).astype(F32)
    return p0, p1, p2


def _norm_forget_kernel(x_ref, g_ref, wf_ref, bf_ref, u_ref, c_ref, carry_ref, *, tiles_per_seq):
    i = pl.program_id(0)
    x = x_ref[...]
    tm = x.shape[0]
    ms = jnp.mean(x * x, axis=-1, keepdims=True)
    u = (x * lax.rsqrt(ms + EPS) * g_ref[...]).astype(BF16)
    u_ref[...] = u

    z = lax.dot_general(u, wf_ref[...].astype(BF16), _NT, preferred_element_type=F32) + bf_ref[...]
    lf = (jnp.minimum(z, 0.0) - jnp.log(1.0 + jnp.exp(-jnp.abs(z)))) * LOG2E

    @pl.when(i % tiles_per_seq == 0)
    def _():
        carry_ref[...] = jnp.zeros_like(carry_ref)

    row = lax.broadcasted_iota(jnp.int32, (tm, tm), 0)
    col = lax.broadcasted_iota(jnp.int32, (tm, tm), 1)
    tri = jnp.where(col <= row, 1.0, 0.0).astype(BF16)
    c = carry_ref[...]
    for part in _split3(lf):
        c = c + jnp.dot(tri, part.astype(BF16), preferred_element_type=F32)
    c_ref[...] = c
    carry_ref[...] = c[tm - 1:tm, :]


def _norm_forget(x2, gain, w_f, b_f, seq, tm):
    m, d = x2.shape
    kern = functools.partial(_norm_forget_kernel, tiles_per_seq=seq // tm)
    return pl.pallas_call(
        kern,
        grid=(m // tm,),
        in_specs=[
            pl.BlockSpec((tm, d), lambda i: (i, 0)),
            pl.BlockSpec((1, d), lambda i: (0, 0)),
            pl.BlockSpec((LANES, d), lambda i: (0, 0)),
            pl.BlockSpec((1, LANES), lambda i: (0, 0)),
        ],
        out_specs=[
            pl.BlockSpec((tm, d), lambda i: (i, 0)),
            pl.BlockSpec((tm, LANES), lambda i: (i, 0)),
        ],
        out_shape=[
            jax.ShapeDtypeStruct((m, d), BF16),
            jax.ShapeDtypeStruct((m, LANES), F32),
        ],
        scratch_shapes=[pltpu.VMEM((1, LANES), F32)],
        compiler_params=_cparams(("arbitrary",)),
        name="norm_forget",
    )(x2, gain, w_f, b_f)


def _proj_chunks(u_ref, w_ref, wb_ref):
    @pl.when(pl.program_id(0) == 0)
    def _():
        wb_ref[...] = w_ref[...].astype(BF16)

    u = u_ref[...]
    for c in range(wb_ref.shape[0] // PROJ_CHUNK):
        cols = slice(c * PROJ_CHUNK, (c + 1) * PROJ_CHUNK)
        yield c * (PROJ_CHUNK // HEAD_DIM), lax.dot_general(u, wb_ref[cols, :], _NT, preferred_element_type=F32)


def _heads_of(h0, y):
    for j in range(PROJ_CHUNK // HEAD_DIM):
        yield h0 + j, y[:, j * HEAD_DIM:(j + 1) * HEAD_DIM]


def _proj_vt_kernel(u_ref, w_ref, o_ref, wb_ref, *, blk):
    tail_row = lax.broadcasted_iota(jnp.int32, (V_ROWS - HEAD_DIM, blk), 0)
    tail = jnp.where(tail_row == 0, 1.0, 0.0).astype(BF16)
    for h0, y in _proj_chunks(u_ref, w_ref, wb_ref):
        for h, yh in _heads_of(h0, y):
            yt = yh.T.astype(BF16)
            for kb in range(yt.shape[1] // blk):
                o_ref[0, h, kb, :HEAD_DIM, :] = yt[:, kb * blk:(kb + 1) * blk]
                o_ref[0, h, kb, HEAD_DIM:, :] = tail


def _proj_silu_kernel(u_ref, w_ref, o_ref, wb_ref):
    for h0, y in _proj_chunks(u_ref, w_ref, wb_ref):
        o_ref[:, h0 * HEAD_DIM:h0 * HEAD_DIM + PROJ_CHUNK] = (y * (1.0 / (1.0 + jnp.exp(-y)))).astype(o_ref.dtype)


def _proj_heads_kernel(u_ref, w_ref, g_ref, o_ref, wb_ref, *, scale):
    gain = g_ref[...] * scale
    for h0, y in _proj_chunks(u_ref, w_ref, wb_ref):
        for h, yh in _heads_of(h0, y):
            ms = jnp.mean(yh * yh, axis=-1, keepdims=True)
            o_ref[0, h, 0] = (yh * lax.rsqrt(ms + EPS) * gain).T.astype(BF16)


def _proj_fox_key_kernel(u_ref, w_ref, g_ref, c_ref, o_ref, wb_ref):
    tm = u_ref.shape[0]
    lane = lax.broadcasted_iota(jnp.int32, (tm, LANES), 1)
    gain = g_ref[...]
    c_all = c_ref[...]
    for h0, y in _proj_chunks(u_ref, w_ref, wb_ref):
        for h, yh in _heads_of(h0, y):
            ms = jnp.mean(yh * yh, axis=-1, keepdims=True)
            o_ref[:, h * AUG:h * AUG + HEAD_DIM] = (yh * lax.rsqrt(ms + EPS) * gain).astype(BF16)
            p0, p1, p2 = _split3(-jnp.broadcast_to(c_all[:, h:h + 1], (tm, LANES)))
            aug = jnp.where(lane == 3, p0, jnp.where(lane == 4, p1, jnp.where(lane == 5, p2, 0.0)))
            aug = jnp.where(lane < 3, 1.0, aug)
            o_ref[:, h * AUG + HEAD_DIM:(h + 1) * AUG] = aug.astype(BF16)


def _proj_diff_kernel(u_ref, w_ref, g_ref, o_ref, wb_ref, *, scale, transposed):
    tm = u_ref.shape[0]
    lane = lax.broadcasted_iota(jnp.int32, (tm, LANES), 1)
    lo = lane < DIFF_QK
    gain = g_ref[...] * scale
    for h0, y in _proj_chunks(u_ref, w_ref, wb_ref):
        for h, yh in _heads_of(h0, y):
            sq = yh * yh
            s_lo = jnp.sum(jnp.where(lo, sq, 0.0), axis=-1, keepdims=True)
            s_all = jnp.sum(sq, axis=-1, keepdims=True)
            ms = jnp.where(lo, s_lo, s_all - s_lo) * (1.0 / DIFF_QK)
            normed = yh * lax.rsqrt(ms + EPS) * gain
            if transposed:
                o_ref[0, h, 0] = normed.T.astype(BF16)
            else:
                o_ref[:, h * HEAD_DIM:(h + 1) * HEAD_DIM] = normed.astype(BF16)


def _proj(kern, u, w_t, layer, section, extra, extra_specs, out_spec, out_shape, tm, name):
    m, d = u.shape
    n = GROUP_WIDTH
    return pl.pallas_call(
        kern,
        grid=(m // tm,),
        in_specs=[
            pl.BlockSpec((tm, d), lambda i: (i, 0)),
            pl.BlockSpec((None, n, d), lambda i: (layer, section, 0), pipeline_mode=pl.Buffered(1)),
        ] + extra_specs,
        out_specs=out_spec,
        out_shape=out_shape,
        scratch_shapes=[pltpu.VMEM((n, d), BF16)],
        compiler_params=_cparams(("arbitrary",)),
        name=name,
    )(u, w_t, *extra)


def _attn_scratch(blk, width):
    return [
        pltpu.VMEM((blk, width), BF16),
        pltpu.VMEM((1, width), F32),
        pltpu.VMEM((2, blk, width), BF16),
        pltpu.VMEM((2, 1, width), F32),
        pltpu.VMEM((1, width), F32),
        pltpu.VMEM((V_ROWS, width), F32),
    ]


def _query_chunks(blk, n_maps):
    qw = Q_PER_K * blk
    return [slice(lo, lo + QUERY_CHUNK) for lo in range(0, n_maps * qw, QUERY_CHUNK)]


def _attn_pipeline(qi, scores, vt_of, blk, chunks, s_ref, mx_ref, p_ref, al_ref, m_ref, acc_ref):
    n = Q_PER_K * (qi + 1)
    qw = Q_PER_K * blk

    def key_block(t):
        return jnp.maximum(n - 1 - t, 0)

    def ahead(t, lanes):
        return (lanes.start % qw) // blk - (Q_PER_K - 1 - t)

    def qk(t, lanes, far=False):
        st = scores(key_block(t), 2 if far else min(ahead(t, lanes), 2), lanes)
        s_ref[:, lanes] = st.astype(BF16)
        mx_ref[:, lanes] = jnp.max(st, axis=0, keepdims=True)

    def softmax(slot, lanes):
        m_old = m_ref[:, lanes]
        m_new = jnp.maximum(m_old, mx_ref[:, lanes]).astype(BF16)
        m_ref[:, lanes] = m_new.astype(F32)
        al_ref[slot, :, lanes] = jnp.exp2(m_old - m_new.astype(F32))
        p_ref[slot, :, lanes] = jnp.exp2(s_ref[:, lanes] - m_new)

    def pv(t, slot, lanes):
        acc_ref[:, lanes] = al_ref[slot, :, lanes] * acc_ref[:, lanes] + jnp.dot(
            vt_of(key_block(t)), p_ref[slot, :, lanes], preferred_element_type=F32)

    def step(t, slot, with_qk=True, static_t=None):
        for lanes in chunks:
            if static_t is None or ahead(static_t, lanes) >= 0:
                softmax(slot, lanes)
            if static_t is None or ahead(static_t - 1, lanes) >= 0:
                pv(t - 1, 1 - slot, lanes)
            if with_qk and (static_t is None or ahead(static_t + 1, lanes) >= 0):
                qk(t + 1, lanes, far=static_t is None)

    m_ref[...] = jnp.full(m_ref.shape, M_FLOOR, F32)
    acc_ref[...] = jnp.zeros(acc_ref.shape, F32)

    for lanes in chunks:
        if ahead(0, lanes) >= 0:
            qk(0, lanes)
    for t in range(Q_PER_K):
        step(t, t % 2, static_t=t)

    assert Q_PER_K % 2 == 0
    @pl.loop(0, jnp.maximum(Q_PER_K // 2 * qi - 1, 0))
    def _(u):
        step(2 * u + Q_PER_K, 0)
        step(2 * u + Q_PER_K + 1, 1)

    @pl.when(qi >= 1)
    def _():
        step(n - 2, 0)
        step(n - 1, 1, with_qk=False)

    for lanes in chunks:
        pv(n - 1, 1, lanes)


def _fox_attn_kernel(q_ref, c_ref, k_ref, vt_ref, gate_ref, o_ref, *scratch, blk):
    qw = Q_PER_K * blk
    key = lax.broadcasted_iota(jnp.int32, (blk, blk), 0)
    qry = lax.broadcasted_iota(jnp.int32, (blk, blk), 1)
    causal = key <= qry
    dim = lax.broadcasted_iota(jnp.int32, (AUG - HEAD_DIM, qw), 0)
    chunks = _query_chunks(blk, 1)

    @pl.loop(0, q_ref.shape[2])
    def _(qi):
        p0, p1, p2 = _split3(c_ref[0, 0, qi])
        decay = jnp.where(dim == 0, p0, jnp.where(dim == 1, p1, jnp.where(dim == 2, p2, 0.0)))
        decay = jnp.where((dim >= 3) & (dim < 6), 1.0, decay)
        qt = jnp.concatenate([q_ref[0, 0, qi], decay.astype(BF16)], axis=0)

        def scores(kb, ahead, lanes):
            k = k_ref[0, pl.ds(pl.multiple_of(kb * blk, blk), blk), :]
            st = jnp.dot(k, qt[:, lanes], preferred_element_type=F32)
            if ahead == 0:
                within = slice(lanes.start % blk, lanes.start % blk + QUERY_CHUNK)
                return jnp.where(causal[:, within], st, MASKED)
            return st

        _attn_pipeline(qi, scores, lambda kb: vt_ref[0, 0, kb], blk, chunks, *scratch)
        acc = scratch[-1][...]
        ot = acc[:HEAD_DIM] / acc[HEAD_DIM:HEAD_DIM + 1]
        rows = pl.ds(pl.multiple_of(qi * qw, qw), qw)
        o_ref[0, rows, :] = (ot.T * gate_ref[0, rows, :]).astype(o_ref.dtype)


def _fox_attn(qt, c_rows, k, vt, gate, blk):
    b, s, _ = k.shape
    qw = Q_PER_K * blk
    head = lambda *block: pl.BlockSpec((1, 1) + block, lambda bi, h: (bi, h) + (0,) * len(block))
    cols = lambda width: pl.BlockSpec((1, s, width), lambda bi, h: (bi, 0, h))
    return pl.pallas_call(
        functools.partial(_fox_attn_kernel, blk=blk),
        grid=(b, N_HEADS),
        in_specs=[
            head(s // qw, HEAD_DIM, qw),
            head(s // qw, 1, qw),
            cols(AUG),
            head(s // blk, V_ROWS, blk),
            cols(HEAD_DIM),
        ],
        out_specs=cols(HEAD_DIM),
        out_shape=jax.ShapeDtypeStruct((b, s, GROUP_WIDTH), BF16),
        scratch_shapes=_attn_scratch(blk, qw),
        compiler_params=_cparams(("arbitrary", "arbitrary")),
        name="fox_attn",
    )(qt, c_rows, k, vt, gate)


def _diff_attn_kernel(q_ref, k_ref, vt_ref, gate_ref, table_ref, lam_ref, onorm_ref, o_ref, bias_ref,
                      *scratch, blk, lambda_init):
    qw = Q_PER_K * blk
    shifted = pltpu.roll(jnp.broadcast_to(table_ref[0], (blk, 2 * blk)), 0, 1, stride=1, stride_axis=0)
    key = lax.broadcasted_iota(jnp.int32, (blk, blk), 0)
    qry = lax.broadcasted_iota(jnp.int32, (blk, blk), 1)
    bias_ref[0] = jnp.where(key <= qry, shifted[:, :blk], MASKED)
    bias_ref[1] = shifted[:, blk:]

    lam_p = lam_ref[...]
    lam = (jnp.exp(jnp.sum(lam_p[0:1] * lam_p[1:2], axis=-1, keepdims=True))
           - jnp.exp(jnp.sum(lam_p[2:3] * lam_p[3:4], axis=-1, keepdims=True)) + lambda_init)
    zeros = jnp.zeros((DIFF_QK, qw), BF16)
    chunks = _query_chunks(blk, 2)

    @pl.loop(0, q_ref.shape[2])
    def _(qi):
        qt = q_ref[0, 0, qi]
        qzt = jnp.concatenate([jnp.concatenate([qt[:DIFF_QK], zeros], axis=0),
                               jnp.concatenate([zeros, qt[DIFF_QK:]], axis=0)], axis=1)

        def scores(kb, ahead, lanes):
            k = k_ref[0, pl.ds(pl.multiple_of(kb * blk, blk), blk), :]
            st = jnp.dot(k, qzt[:, lanes], preferred_element_type=F32)
            if ahead < 2:
                within = slice(lanes.start % blk, lanes.start % blk + QUERY_CHUNK)
                return st + bias_ref[ahead, :, within]
            return st

        _attn_pipeline(qi, scores, lambda kb: vt_ref[0, 0, kb], blk, chunks, *scratch)

        acc = scratch[-1][...]
        ot = acc[:HEAD_DIM] / acc[HEAD_DIM:HEAD_DIM + 1]
        o = (ot[:, :qw] - lam * ot[:, qw:]).T
        ms = jnp.mean(o * o, axis=-1, keepdims=True)
        o = o * lax.rsqrt(ms + EPS) * onorm_ref[...] * (1.0 - lambda_init)
        rows = pl.ds(pl.multiple_of(qi * qw, qw), qw)
        o_ref[0, rows, :] = (o * gate_ref[0, rows, :]).astype(o_ref.dtype)


def _diff_attn(qt, k, vt, gate, bias_table, lam_params, onorm, blk, lambda_init):
    b, s, _ = k.shape
    qw = Q_PER_K * blk
    head = lambda *block: pl.BlockSpec((1, 1) + block, lambda bi, h: (bi, h) + (0,) * len(block))
    cols = lambda width: pl.BlockSpec((1, s, width), lambda bi, h: (bi, 0, h))
    return pl.pallas_call(
        functools.partial(_diff_attn_kernel, blk=blk, lambda_init=lambda_init),
        grid=(b, N_HEADS),
        in_specs=[
            head(s // qw, HEAD_DIM, qw),
            cols(HEAD_DIM),
            head(s // blk, V_ROWS, blk),
            cols(HEAD_DIM),
            pl.BlockSpec((1, 1, 2 * blk), lambda bi, h: (h, 0, 0)),
            pl.BlockSpec((4, DIFF_QK), lambda bi, h: (0, 0)),
            pl.BlockSpec((1, HEAD_DIM), lambda bi, h: (0, 0)),
        ],
        out_specs=cols(HEAD_DIM),
        out_shape=jax.ShapeDtypeStruct((b, s, GROUP_WIDTH), BF16),
        scratch_shapes=[pltpu.VMEM((2, blk, blk), F32)] + _attn_scratch(blk, 2 * qw),
        compiler_params=_cparams(("arbitrary", "arbitrary")),
        name="diff_attn",
    )(qt, k, vt, gate, bias_table, lam_params, onorm)


def _out_ple_kernel(x_ref, ma_ref, mb_ref, p_ref, wo_ref, wp_ref, gn_ref, wg_ref, o_ref):
    half = ma_ref.shape[1]
    h = x_ref[...]
    h = h + jnp.dot(ma_ref[...], wo_ref[:half, :], preferred_element_type=F32)
    h = h + jnp.dot(mb_ref[...], wo_ref[half:, :], preferred_element_type=F32)
    ms = jnp.mean(h * h, axis=-1, keepdims=True)
    hn = (h * lax.rsqrt(ms + EPS) * gn_ref[...]).astype(BF16)
    zg = jnp.dot(hn, wg_ref[...], preferred_element_type=F32)
    g = 1.0 / (1.0 + jnp.exp(-zg))
    e = jnp.dot(p_ref[...].astype(BF16), wp_ref[...], preferred_element_type=F32)
    o_ref[...] = h + g * e


def _out_ple(x2, mix_a, mix_b, p2, w_out, ple_proj, gate_norm, ple_gate, tm):
    m, d = x2.shape
    half = mix_a.shape[1]
    pdim = p2.shape[1]
    const = lambda i: (0, 0)
    single = dict(pipeline_mode=pl.Buffered(1))
    return pl.pallas_call(
        _out_ple_kernel,
        grid=(m // tm,),
        in_specs=[
            pl.BlockSpec((tm, d), lambda i: (i, 0)),
            pl.BlockSpec((tm, half), lambda i: (i, 0)),
            pl.BlockSpec((tm, half), lambda i: (i, 0)),
            pl.BlockSpec((tm, pdim), lambda i: (i, 0)),
            pl.BlockSpec((2 * half, d), const, **single),
            pl.BlockSpec((pdim, d), const, **single),
            pl.BlockSpec((1, d), const),
            pl.BlockSpec((d, d), const, **single),
        ],
        out_specs=pl.BlockSpec((tm, d), lambda i: (i, 0)),
        out_shape=jax.ShapeDtypeStruct((m, d), F32),
        compiler_params=_cparams(("arbitrary",)),
        name="out_ple",
    )(x2, mix_a, mix_b, p2, w_out, ple_proj, gate_norm, ple_gate)


def _t5_bucket(n):
    max_exact = N_BUCKETS // 2
    nf = jnp.maximum(n, 1).astype(F32)
    large = max_exact + (jnp.log(nf / max_exact) / math.log(MAX_DISTANCE / max_exact)
                         * (N_BUCKETS - max_exact)).astype(jnp.int32)
    large = jnp.minimum(large, N_BUCKETS - 1)
    return jnp.where(n < max_exact, n, large)


def _bias_table(rel_bias, blk):
    assert blk >= MAX_DISTANCE
    table = rel_bias.astype(F32)[_t5_bucket(jnp.arange(2 * blk))].T
    table = (table - rel_bias.astype(F32)[N_BUCKETS - 1][:, None]) * LOG2E
    return table[:, None, :]


def kernel(x, p, attn_norm, w_in, b_forget, fox_q_norm, fox_k_norm, diff_q_norm, diff_k_norm,
           lambda_q1, lambda_k1, lambda_q2, lambda_k2, diff_out_norm, w_out, rel_bias,
           ple_proj, ple_gate_norm, ple_gate):
    b, s, d = x.shape
    depth = w_in.shape[0]
    gw = GROUP_WIDTH
    blk = 512
    tm = 1024
    bias_table = _bias_table(rel_bias, blk)
    w_t = jnp.swapaxes(w_in.astype(F32), 1, 2)

    h = x.reshape(b * s, d)
    for i in range(depth):
        lambda_init = 0.8 - 0.6 * math.exp(-0.3 * i)
        w_f = jnp.pad(w_t[i, 8 * gw:, :], ((0, LANES - N_HEADS), (0, 0)))
        b_f = jnp.pad(b_forget[i].astype(F32), (0, LANES - N_HEADS)).reshape(1, LANES)
        u, c2 = _norm_forget(h, attn_norm[i].astype(F32).reshape(1, d), w_f, b_f, s, 512)

        row = lambda a: a.astype(F32).reshape(1, -1)
        vec_spec = lambda n: pl.BlockSpec((1, n), lambda r: (0, 0))
        c_spec = pl.BlockSpec((tm, LANES), lambda r: (r, 0))
        fox_scale = HEAD_DIM ** -0.5 * LOG2E
        diff_scale = DIFF_QK ** -0.5 * LOG2E

        tiles = s // tm
        flat = lambda width, dtype: (pl.BlockSpec((tm, width), lambda r: (r, 0)),
                                     jax.ShapeDtypeStruct((b * s, width), dtype))
        qw = Q_PER_K * blk
        per_q = qw // tm
        heads_t = (pl.BlockSpec((1, N_HEADS, 1, HEAD_DIM, tm),
                                lambda r: (r // tiles, 0, (r % tiles) // per_q, 0, (r % tiles) % per_q)),
                   jax.ShapeDtypeStruct((b, N_HEADS, s // qw, HEAD_DIM, qw), BF16))
        v_t = (pl.BlockSpec((1, N_HEADS, tm // blk, V_ROWS, blk), lambda r: (r // tiles, 0, r % tiles, 0, 0)),
               jax.ShapeDtypeStruct((b, N_HEADS, s // blk, V_ROWS, blk), BF16))
        proj = lambda kern, section, extra, specs, out, name: _proj(
            kern, u, w_t, i, section, extra, specs, out[0], out[1], tm, name)
        vt_kernel = functools.partial(_proj_vt_kernel, blk=blk)

        qa = proj(functools.partial(_proj_heads_kernel, scale=fox_scale), 0,
                  [row(fox_q_norm[i])], [vec_spec(HEAD_DIM)], heads_t, "proj_fox_q")
        ka = proj(_proj_fox_key_kernel, 1, [row(fox_k_norm[i]), c2], [vec_spec(HEAD_DIM), c_spec],
                  flat(N_HEADS * AUG, BF16), "proj_fox_k")
        va = proj(vt_kernel, 2, [], [], v_t, "proj_fox_v")
        ga = proj(_proj_silu_kernel, 3, [], [], flat(gw, F32), "proj_fox_gate")
        dq_gain = row(jnp.tile(diff_q_norm[i], 2))
        dk_gain = row(jnp.tile(diff_k_norm[i], 2))
        qb = proj(functools.partial(_proj_diff_kernel, scale=diff_scale, transposed=True), 4,
                  [dq_gain], [vec_spec(HEAD_DIM)], heads_t, "proj_diff_q")
        kb = proj(functools.partial(_proj_diff_kernel, scale=1.0, transposed=False), 5,
                  [dk_gain], [vec_spec(HEAD_DIM)], flat(gw, BF16), "proj_diff_k")
        vb = proj(vt_kernel, 6, [], [], v_t, "proj_diff_v")
        gb = proj(_proj_silu_kernel, 7, [], [], flat(gw, F32), "proj_diff_gate")

        to3 = lambda a: a.reshape(b, s, a.shape[-1])
        c_rows = (c2[:, :N_HEADS].reshape(b, s // qw, qw, N_HEADS).transpose(0, 3, 1, 2)
                  .reshape(b, N_HEADS, s // qw, 1, qw))
        mix_a = _fox_attn(qa, c_rows, to3(ka), va, to3(ga), blk)
        lam_params = jnp.stack([lambda_q1[i], lambda_k1[i], lambda_q2[i], lambda_k2[i]]).astype(F32)
        mix_b = _diff_attn(qb, to3(kb), vb, to3(gb), bias_table, lam_params,
                           row(diff_out_norm[i]), blk, lambda_init)

        h = _out_ple(h, mix_a.reshape(b * s, gw), mix_b.reshape(b * s, gw),
                     p[i].reshape(b * s, -1), w_out[i].astype(BF16), ple_proj[i].astype(BF16),
                     row(ple_gate_norm[i]), ple_gate[i].astype(BF16), 256)
    return h.reshape(b, s, d)
```

```python
import functools
import math

import jax
import jax.numpy as jnp
from jax import lax
from jax.experimental import pallas as pl
from jax.experimental.pallas import tpu as pltpu

F32 = jnp.float32
BF16 = jnp.bfloat16

HEAD_DIM = 128
N_HEADS = 8
DIFF_QK = HEAD_DIM // 2
GROUP_WIDTH = N_HEADS * HEAD_DIM
N_BUCKETS = 32
MAX_DISTANCE = 128
EPS = 1e-6
LOG2E = math.log2(math.e)
MASKED = -3e38
M_FLOOR = -(2.0 ** 100)
Q_PER_K = 4
QUERY_CHUNK = 256
PROJ_CHUNK = 2 * HEAD_DIM
OUT_CHUNK = 512
NORM_ROWS = 256
LANES = 128
AUG = 2 * HEAD_DIM
V_ROWS = HEAD_DIM + 16
VMEM_LIMIT = 56 * 1024 * 1024


def _cparams(sem):
    return pltpu.CompilerParams(dimension_semantics=sem, vmem_limit_bytes=VMEM_LIMIT)


_NT = (((1,), (1,)), ((), ()))


def _split3(v):
    p0 = v.astype(BF16).astype(F32)
    r1 = v - p0
    p1 = r1.astype(BF16).astype(F32)
    p2 = (r1 - p1).astype(BF16).astype(F32)
    return p0, p1, p2


def _norm_forget_kernel(x_ref, g_ref, wf_ref, bf_ref, u_ref, c_ref, carry_ref, *, tiles_per_seq):
    i = pl.program_id(0)

    @pl.when(i % tiles_per_seq == 0)
    def _():
        carry_ref[...] = jnp.zeros_like(carry_ref)

    sub = NORM_ROWS
    row = lax.broadcasted_iota(jnp.int32, (sub, sub), 0)
    col = lax.broadcasted_iota(jnp.int32, (sub, sub), 1)
    tri = jnp.where(col <= row, 1.0, 0.0).astype(BF16)
    wf = wf_ref[...].astype(BF16)
    carry = carry_ref[...]
    for r in range(x_ref.shape[0] // sub):
        rows = slice(r * sub, (r + 1) * sub)
        x = x_ref[rows, :]
        ms = jnp.mean(x * x, axis=-1, keepdims=True)
        u = (x * lax.rsqrt(ms + EPS) * g_ref[...]).astype(BF16)
        u_ref[rows, :] = u

        z = lax.dot_general(u, wf, _NT, preferred_element_type=F32) + bf_ref[...]
        lf = (jnp.minimum(z, 0.0) - jnp.log(1.0 + jnp.exp(-jnp.abs(z)))) * LOG2E
        c = carry
        for part in _split3(lf):
            c = c + jnp.dot(tri, part.astype(BF16), preferred_element_type=F32)
        c_ref[rows, :] = c
        carry = c[sub - 1:sub, :]
    carry_ref[...] = carry


def _norm_forget(x2, gain, w_f, b_f, seq, tm):
    m, d = x2.shape
    kern = functools.partial(_norm_forget_kernel, tiles_per_seq=seq // tm)
    return pl.pallas_call(
        kern,
        grid=(m // tm,),
        in_specs=[
            pl.BlockSpec((tm, d), lambda i: (i, 0)),
            pl.BlockSpec((1, d), lambda i: (0, 0)),
            pl.BlockSpec((LANES, d), lambda i: (0, 0)),
            pl.BlockSpec((1, LANES), lambda i: (0, 0)),
        ],
        out_specs=[
            pl.BlockSpec((tm, d), lambda i: (i, 0)),
            pl.BlockSpec((tm, LANES), lambda i: (i, 0)),
        ],
        out_shape=[
            jax.ShapeDtypeStruct((m, d), BF16),
            jax.ShapeDtypeStruct((m, LANES), F32),
        ],
        scratch_shapes=[pltpu.VMEM((1, LANES), F32)],
        compiler_params=_cparams(("arbitrary",)),
        name="norm_forget",
    )(x2, gain, w_f, b_f)


def _proj_chunks(u_ref, w_ref, wb_ref):
    @pl.when(pl.program_id(0) == 0)
    def _():
        wb_ref[...] = w_ref[...].astype(BF16)

    u = u_ref[...]
    for c in range(wb_ref.shape[0] // PROJ_CHUNK):
        cols = slice(c * PROJ_CHUNK, (c + 1) * PROJ_CHUNK)
        yield c * (PROJ_CHUNK // HEAD_DIM), lax.dot_general(u, wb_ref[cols, :], _NT, preferred_element_type=F32)


def _heads_of(h0, y):
    for j in range(PROJ_CHUNK // HEAD_DIM):
        yield h0 + j, y[:, j * HEAD_DIM:(j + 1) * HEAD_DIM]


def _proj_vt_kernel(u_ref, w_ref, o_ref, wb_ref, *, blk):
    tail_row = lax.broadcasted_iota(jnp.int32, (V_ROWS - HEAD_DIM, blk), 0)
    tail = jnp.where(tail_row == 0, 1.0, 0.0).astype(BF16)
    for h0, y in _proj_chunks(u_ref, w_ref, wb_ref):
        for h, yh in _heads_of(h0, y):
            yt = yh.T.astype(BF16)
            for kb in range(yt.shape[1] // blk):
                o_ref[0, h, kb, :HEAD_DIM, :] = yt[:, kb * blk:(kb + 1) * blk]
                o_ref[0, h, kb, HEAD_DIM:, :] = tail


def _proj_silu_kernel(u_ref, w_ref, o_ref, wb_ref):
    for h0, y in _proj_chunks(u_ref, w_ref, wb_ref):
        o_ref[:, h0 * HEAD_DIM:h0 * HEAD_DIM + PROJ_CHUNK] = (y * (1.0 / (1.0 + jnp.exp(-y)))).astype(o_ref.dtype)


def _proj_heads_kernel(u_ref, w_ref, g_ref, o_ref, wb_ref, *, scale):
    gain = g_ref[...] * scale
    for h0, y in _proj_chunks(u_ref, w_ref, wb_ref):
        for h, yh in _heads_of(h0, y):
            ms = jnp.mean(yh * yh, axis=-1, keepdims=True)
            o_ref[0, h, 0] = (yh * lax.rsqrt(ms + EPS) * gain).T.astype(BF16)


def _proj_fox_key_kernel(u_ref, w_ref, g_ref, c_ref, o_ref, wb_ref):
    tm = u_ref.shape[0]
    lane = lax.broadcasted_iota(jnp.int32, (tm, LANES), 1)
    gain = g_ref[...]
    c_all = c_ref[...]
    for h0, y in _proj_chunks(u_ref, w_ref, wb_ref):
        for h, yh in _heads_of(h0, y):
            ms = jnp.mean(yh * yh, axis=-1, keepdims=True)
            o_ref[:, h * AUG:h * AUG + HEAD_DIM] = (yh * lax.rsqrt(ms + EPS) * gain).astype(BF16)
            p0, p1, p2 = _split3(-jnp.broadcast_to(c_all[:, h:h + 1], (tm, LANES)))
            aug = jnp.where(lane == 3, p0, jnp.where(lane == 4, p1, jnp.where(lane == 5, p2, 0.0)))
            aug = jnp.where(lane < 3, 1.0, aug)
            o_ref[:, h * AUG + HEAD_DIM:(h + 1) * AUG] = aug.astype(BF16)


def _proj_diff_kernel(u_ref, w_ref, g_ref, o_ref, wb_ref, *, scale, transposed):
    tm = u_ref.shape[0]
    lane = lax.broadcasted_iota(jnp.int32, (tm, LANES), 1)
    lo = lane < DIFF_QK
    gain = g_ref[...] * scale
    for h0, y in _proj_chunks(u_ref, w_ref, wb_ref):
        for h, yh in _heads_of(h0, y):
            sq = yh * yh
            s_lo = jnp.sum(jnp.where(lo, sq, 0.0), axis=-1, keepdims=True)
            s_all = jnp.sum(sq, axis=-1, keepdims=True)
            ms = jnp.where(lo, s_lo, s_all - s_lo) * (1.0 / DIFF_QK)
            normed = yh * lax.rsqrt(ms + EPS) * gain
            if transposed:
                o_ref[0, h, 0] = normed.T.astype(BF16)
            else:
                o_ref[:, h * HEAD_DIM:(h + 1) * HEAD_DIM] = normed.astype(BF16)


def _proj(kern, u, w_t, layer, section, extra, extra_specs, out_spec, out_shape, tm, name):
    m, d = u.shape
    n = GROUP_WIDTH
    return pl.pallas_call(
        kern,
        grid=(m // tm,),
        in_specs=[
            pl.BlockSpec((tm, d), lambda i: (i, 0)),
            pl.BlockSpec((None, n, d), lambda i: (layer, section, 0), pipeline_mode=pl.Buffered(1)),
        ] + extra_specs,
        out_specs=out_spec,
        out_shape=out_shape,
        scratch_shapes=[pltpu.VMEM((n, d), BF16)],
        compiler_params=_cparams(("arbitrary",)),
        name=name,
    )(u, w_t, *extra)


def _attn_scratch(blk, width):
    return [
        pltpu.VMEM((blk, width), BF16),
        pltpu.VMEM((1, width), F32),
        pltpu.VMEM((2, blk, width), BF16),
        pltpu.VMEM((2, 1, width), F32),
        pltpu.VMEM((1, width), F32),
        pltpu.VMEM((V_ROWS, width), F32),
    ]


def _query_chunks(blk, n_maps):
    qw = Q_PER_K * blk
    return [slice(lo, lo + QUERY_CHUNK) for lo in range(0, n_maps * qw, QUERY_CHUNK)]


def _attn_pipeline(qi, scores, vt_of, blk, chunks, s_ref, mx_ref, p_ref, al_ref, m_ref, acc_ref):
    n = Q_PER_K * (qi + 1)
    qw = Q_PER_K * blk

    def key_block(t):
        return jnp.maximum(n - 1 - t, 0)

    def ahead(t, lanes):
        return (lanes.start % qw) // blk - (Q_PER_K - 1 - t)

    def qk(t, lanes, far=False):
        st = scores(key_block(t), 2 if far else min(ahead(t, lanes), 2), lanes)
        s_ref[:, lanes] = st.astype(BF16)
        mx_ref[:, lanes] = jnp.max(st, axis=0, keepdims=True)

    def softmax(slot, lanes):
        m_old = m_ref[:, lanes]
        m_new = jnp.maximum(m_old, mx_ref[:, lanes]).astype(BF16)
        m_ref[:, lanes] = m_new.astype(F32)
        al_ref[slot, :, lanes] = jnp.exp2(m_old - m_new.astype(F32))
        p_ref[slot, :, lanes] = jnp.exp2(s_ref[:, lanes] - m_new)

    def pv(t, slot, lanes):
        acc_ref[:, lanes] = al_ref[slot, :, lanes] * acc_ref[:, lanes] + jnp.dot(
            vt_of(key_block(t)), p_ref[slot, :, lanes], preferred_element_type=F32)

    def step(t, slot, with_qk=True, static_t=None):
        for lanes in chunks:
            if static_t is None or ahead(static_t, lanes) >= 0:
                softmax(slot, lanes)
            if static_t is None or ahead(static_t - 1, lanes) >= 0:
                pv(t - 1, 1 - slot, lanes)
            if with_qk and (static_t is None or ahead(static_t + 1, lanes) >= 0):
                qk(t + 1, lanes, far=static_t is None)

    m_ref[...] = jnp.full(m_ref.shape, M_FLOOR, F32)
    acc_ref[...] = jnp.zeros(acc_ref.shape, F32)

    for lanes in chunks:
        if ahead(0, lanes) >= 0:
            qk(0, lanes)
    for t in range(Q_PER_K):
        step(t, t % 2, static_t=t)

    assert Q_PER_K % 2 == 0
    @pl.loop(0, jnp.maximum(Q_PER_K // 2 * qi - 1, 0))
    def _(u):
        step(2 * u + Q_PER_K, 0)
        step(2 * u + Q_PER_K + 1, 1)

    @pl.when(qi >= 1)
    def _():
        step(n - 2, 0)
        step(n - 1, 1, with_qk=False)

    for lanes in chunks:
        pv(n - 1, 1, lanes)


def _fox_attn_kernel(q_ref, c_ref, k_ref, vt_ref, gate_ref, o_ref, *scratch, blk):
    qw = Q_PER_K * blk
    key = lax.broadcasted_iota(jnp.int32, (blk, blk), 0)
    qry = lax.broadcasted_iota(jnp.int32, (blk, blk), 1)
    causal = key <= qry
    dim = lax.broadcasted_iota(jnp.int32, (AUG - HEAD_DIM, qw), 0)
    chunks = _query_chunks(blk, 1)

    @pl.loop(0, q_ref.shape[2])
    def _(qi):
        p0, p1, p2 = _split3(c_ref[0, 0, qi])
        decay = jnp.where(dim == 0, p0, jnp.where(dim == 1, p1, jnp.where(dim == 2, p2, 0.0)))
        decay = jnp.where((dim >= 3) & (dim < 6), 1.0, decay)
        qt = jnp.concatenate([q_ref[0, 0, qi], decay.astype(BF16)], axis=0)

        def scores(kb, ahead, lanes):
            k = k_ref[0, pl.ds(pl.multiple_of(kb * blk, blk), blk), :]
            st = jnp.dot(k, qt[:, lanes], preferred_element_type=F32)
            if ahead == 0:
                within = slice(lanes.start % blk, lanes.start % blk + QUERY_CHUNK)
                return jnp.where(causal[:, within], st, MASKED)
            return st

        _attn_pipeline(qi, scores, lambda kb: vt_ref[0, 0, kb], blk, chunks, *scratch)
        acc = scratch[-1][...]
        ot = acc[:HEAD_DIM] / acc[HEAD_DIM:HEAD_DIM + 1]
        rows = pl.ds(pl.multiple_of(qi * qw, qw), qw)
        o_ref[0, rows, :] = (ot.T * gate_ref[0, rows, :]).astype(o_ref.dtype)


def _fox_attn(qt, c_rows, k, vt, gate, blk):
    b, s, _ = k.shape
    qw = Q_PER_K * blk
    head = lambda *block: pl.BlockSpec((1, 1) + block, lambda bi, h: (bi, h) + (0,) * len(block))
    cols = lambda width: pl.BlockSpec((1, s, width), lambda bi, h: (bi, 0, h))
    return pl.pallas_call(
        functools.partial(_fox_attn_kernel, blk=blk),
        grid=(b, N_HEADS),
        in_specs=[
            head(s // qw, HEAD_DIM, qw),
            head(s // qw, 1, qw),
            cols(AUG),
            head(s // blk, V_ROWS, blk),
            cols(HEAD_DIM),
        ],
        out_specs=cols(HEAD_DIM),
        out_shape=jax.ShapeDtypeStruct((b, s, GROUP_WIDTH), BF16),
        scratch_shapes=_attn_scratch(blk, qw),
        compiler_params=_cparams(("arbitrary", "arbitrary")),
        name="fox_attn",
    )(qt, c_rows, k, vt, gate)


def _diff_attn_kernel(q_ref, k_ref, vt_ref, gate_ref, table_ref, lam_ref, onorm_ref, o_ref, bias_ref,
                      *scratch, blk, lambda_init):
    qw = Q_PER_K * blk
    shifted = pltpu.roll(jnp.broadcast_to(table_ref[0], (blk, 2 * blk)), 0, 1, stride=1, stride_axis=0)
    key = lax.broadcasted_iota(jnp.int32, (blk, blk), 0)
    qry = lax.broadcasted_iota(jnp.int32, (blk, blk), 1)
    bias_ref[0] = jnp.where(key <= qry, shifted[:, :blk], MASKED)
    bias_ref[1] = shifted[:, blk:]

    lam_p = lam_ref[...]
    lam = (jnp.exp(jnp.sum(lam_p[0:1] * lam_p[1:2], axis=-1, keepdims=True))
           - jnp.exp(jnp.sum(lam_p[2:3] * lam_p[3:4], axis=-1, keepdims=True)) + lambda_init)
    zeros = jnp.zeros((DIFF_QK, qw), BF16)
    chunks = _query_chunks(blk, 2)

    @pl.loop(0, q_ref.shape[2])
    def _(qi):
        qt = q_ref[0, 0, qi]
        qzt = jnp.concatenate([jnp.concatenate([qt[:DIFF_QK], zeros], axis=0),
                               jnp.concatenate([zeros, qt[DIFF_QK:]], axis=0)], axis=1)

        def scores(kb, ahead, lanes):
            k = k_ref[0, pl.ds(pl.multiple_of(kb * blk, blk), blk), :]
            st = jnp.dot(k, qzt[:, lanes], preferred_element_type=F32)
            if ahead < 2:
                within = slice(lanes.start % blk, lanes.start % blk + QUERY_CHUNK)
                return st + bias_ref[ahead, :, within]
            return st

        _attn_pipeline(qi, scores, lambda kb: vt_ref[0, 0, kb], blk, chunks, *scratch)

        acc = scratch[-1][...]
        ot = acc[:HEAD_DIM] / acc[HEAD_DIM:HEAD_DIM + 1]
        o = (ot[:, :qw] - lam * ot[:, qw:]).T
        ms = jnp.mean(o * o, axis=-1, keepdims=True)
        o = o * lax.rsqrt(ms + EPS) * onorm_ref[...] * (1.0 - lambda_init)
        rows = pl.ds(pl.multiple_of(qi * qw, qw), qw)
        o_ref[0, rows, :] = (o * gate_ref[0, rows, :]).astype(o_ref.dtype)


def _diff_attn(qt, k, vt, gate, bias_table, lam_params, onorm, blk, lambda_init):
    b, s, _ = k.shape
    qw = Q_PER_K * blk
    head = lambda *block: pl.BlockSpec((1, 1) + block, lambda bi, h: (bi, h) + (0,) * len(block))
    cols = lambda width: pl.BlockSpec((1, s, width), lambda bi, h: (bi, 0, h))
    return pl.pallas_call(
        functools.partial(_diff_attn_kernel, blk=blk, lambda_init=lambda_init),
        grid=(b, N_HEADS),
        in_specs=[
            head(s // qw, HEAD_DIM, qw),
            cols(HEAD_DIM),
            head(s // blk, V_ROWS, blk),
            cols(HEAD_DIM),
            pl.BlockSpec((1, 1, 2 * blk), lambda bi, h: (h, 0, 0)),
            pl.BlockSpec((4, DIFF_QK), lambda bi, h: (0, 0)),
            pl.BlockSpec((1, HEAD_DIM), lambda bi, h: (0, 0)),
        ],
        out_specs=cols(HEAD_DIM),
        out_shape=jax.ShapeDtypeStruct((b, s, GROUP_WIDTH), BF16),
        scratch_shapes=[pltpu.VMEM((2, blk, blk), F32)] + _attn_scratch(blk, 2 * qw),
        compiler_params=_cparams(("arbitrary", "arbitrary")),
        name="diff_attn",
    )(qt, k, vt, gate, bias_table, lam_params, onorm)


def _out_ple_kernel(x_ref, ma_ref, mb_ref, p_ref, wo_ref, wp_ref, gn_ref, wg_ref, o_ref, h_ref):
    half = ma_ref.shape[1]
    d = o_ref.shape[1]
    chunks = [slice(lo, lo + OUT_CHUNK) for lo in range(0, d, OUT_CHUNK)]

    ma, mb = ma_ref[...], mb_ref[...]
    sum_sq = jnp.zeros((ma.shape[0], 1), F32)
    for cols in chunks:
        hc = (x_ref[:, cols] + jnp.dot(ma, wo_ref[:half, cols], preferred_element_type=F32)
              + jnp.dot(mb, wo_ref[half:, cols], preferred_element_type=F32))
        h_ref[:, cols] = hc
        sum_sq = sum_sq + jnp.sum(hc * hc, axis=-1, keepdims=True)

    hn = (h_ref[...] * lax.rsqrt(sum_sq * (1.0 / d) + EPS) * gn_ref[...]).astype(BF16)
    pb = p_ref[...].astype(BF16)
    for cols in chunks:
        zg = jnp.dot(hn, wg_ref[:, cols], preferred_element_type=F32)
        e = jnp.dot(pb, wp_ref[:, cols], preferred_element_type=F32)
        o_ref[:, cols] = h_ref[:, cols] + e * (1.0 / (1.0 + jnp.exp(-zg)))


def _out_ple(x2, mix_a, mix_b, p2, w_out, ple_proj, gate_norm, ple_gate, tm):
    m, d = x2.shape
    half = mix_a.shape[1]
    pdim = p2.shape[1]
    const = lambda i: (0, 0)
    single = dict(pipeline_mode=pl.Buffered(1))
    return pl.pallas_call(
        _out_ple_kernel,
        grid=(m // tm,),
        in_specs=[
            pl.BlockSpec((tm, d), lambda i: (i, 0)),
            pl.BlockSpec((tm, half), lambda i: (i, 0)),
            pl.BlockSpec((tm, half), lambda i: (i, 0)),
            pl.BlockSpec((tm, pdim), lambda i: (i, 0)),
            pl.BlockSpec((2 * half, d), const, **single),
            pl.BlockSpec((pdim, d), const, **single),
            pl.BlockSpec((1, d), const),
            pl.BlockSpec((d, d), const, **single),
        ],
        out_specs=pl.BlockSpec((tm, d), lambda i: (i, 0)),
        out_shape=jax.ShapeDtypeStruct((m, d), F32),
        scratch_shapes=[pltpu.VMEM((tm, d), F32)],
        compiler_params=_cparams(("arbitrary",)),
        name="out_ple",
    )(x2, mix_a, mix_b, p2, w_out, ple_proj, gate_norm, ple_gate)


def _t5_bucket(n):
    max_exact = N_BUCKETS // 2
    nf = jnp.maximum(n, 1).astype(F32)
    large = max_exact + (jnp.log(nf / max_exact) / math.log(MAX_DISTANCE / max_exact)
                         * (N_BUCKETS - max_exact)).astype(jnp.int32)
    large = jnp.minimum(large, N_BUCKETS - 1)
    return jnp.where(n < max_exact, n, large)


def _bias_table(rel_bias, blk):
    assert blk >= MAX_DISTANCE
    table = rel_bias.astype(F32)[_t5_bucket(jnp.arange(2 * blk))].T
    table = (table - rel_bias.astype(F32)[N_BUCKETS - 1][:, None]) * LOG2E
    return table[:, None, :]


def kernel(x, p, attn_norm, w_in, b_forget, fox_q_norm, fox_k_norm, diff_q_norm, diff_k_norm,
           lambda_q1, lambda_k1, lambda_q2, lambda_k2, diff_out_norm, w_out, rel_bias,
           ple_proj, ple_gate_norm, ple_gate):
    b, s, d = x.shape
    depth = w_in.shape[0]
    gw = GROUP_WIDTH
    blk = 512
    tm = 1024
    bias_table = _bias_table(rel_bias, blk)
    w_t = jnp.swapaxes(w_in.astype(F32), 1, 2)

    h = x.reshape(b * s, d)
    for i in range(depth):
        lambda_init = 0.8 - 0.6 * math.exp(-0.3 * i)
        w_f = jnp.pad(w_t[i, 8 * gw:, :], ((0, LANES - N_HEADS), (0, 0)))
        b_f = jnp.pad(b_forget[i].astype(F32), (0, LANES - N_HEADS)).reshape(1, LANES)
        u, c2 = _norm_forget(h, attn_norm[i].astype(F32).reshape(1, d), w_f, b_f, s, tm)

        row = lambda a: a.astype(F32).reshape(1, -1)
        vec_spec = lambda n: pl.BlockSpec((1, n), lambda r: (0, 0))
        c_spec = pl.BlockSpec((tm, LANES), lambda r: (r, 0))
        fox_scale = HEAD_DIM ** -0.5 * LOG2E
        diff_scale = DIFF_QK ** -0.5 * LOG2E

        tiles = s // tm
        flat = lambda width, dtype: (pl.BlockSpec((tm, width), lambda r: (r, 0)),
                                     jax.ShapeDtypeStruct((b * s, width), dtype))
        qw = Q_PER_K * blk
        per_q = qw // tm
        heads_t = (pl.BlockSpec((1, N_HEADS, 1, HEAD_DIM, tm),
                                lambda r: (r // tiles, 0, (r % tiles) // per_q, 0, (r % tiles) % per_q)),
                   jax.ShapeDtypeStruct((b, N_HEADS, s // qw, HEAD_DIM, qw), BF16))
        v_t = (pl.BlockSpec((1, N_HEADS, tm // blk, V_ROWS, blk), lambda r: (r // tiles, 0, r % tiles, 0, 0)),
               jax.ShapeDtypeStruct((b, N_HEADS, s // blk, V_ROWS, blk), BF16))
        proj = lambda kern, section, extra, specs, out, name: _proj(
            kern, u, w_t, i, section, extra, specs, out[0], out[1], tm, name)
        vt_kernel = functools.partial(_proj_vt_kernel, blk=blk)

        qa = proj(functools.partial(_proj_heads_kernel, scale=fox_scale), 0,
                  [row(fox_q_norm[i])], [vec_spec(HEAD_DIM)], heads_t, "proj_fox_q")
        ka = proj(_proj_fox_key_kernel, 1, [row(fox_k_norm[i]), c2], [vec_spec(HEAD_DIM), c_spec],
                  flat(N_HEADS * AUG, BF16), "proj_fox_k")
        va = proj(vt_kernel, 2, [], [], v_t, "proj_fox_v")
        ga = proj(_proj_silu_kernel, 3, [], [], flat(gw, F32), "proj_fox_gate")
        dq_gain = row(jnp.tile(diff_q_norm[i], 2))
        dk_gain = row(jnp.tile(diff_k_norm[i], 2))
        qb = proj(functools.partial(_proj_diff_kernel, scale=diff_scale, transposed=True), 4,
                  [dq_gain], [vec_spec(HEAD_DIM)], heads_t, "proj_diff_q")
        kb = proj(functools.partial(_proj_diff_kernel, scale=1.0, transposed=False), 5,
                  [dk_gain], [vec_spec(HEAD_DIM)], flat(gw, BF16), "proj_diff_k")
        vb = proj(vt_kernel, 6, [], [], v_t, "proj_diff_v")
        gb = proj(_proj_silu_kernel, 7, [], [], flat(gw, F32), "proj_diff_gate")

        to3 = lambda a: a.reshape(b, s, a.shape[-1])
        c_rows = (c2[:, :N_HEADS].reshape(b, s // qw, qw, N_HEADS).transpose(0, 3, 1, 2)
                  .reshape(b, N_HEADS, s // qw, 1, qw))
        mix_a = _fox_attn(qa, c_rows, to3(ka), va, to3(ga), blk)
        lam_params = jnp.stack([lambda_q1[i], lambda_k1[i], lambda_q2[i], lambda_k2[i]]).astype(F32)
        mix_b = _diff_attn(qb, to3(kb), vb, to3(gb), bias_table, lam_params,
                           row(diff_out_norm[i]), blk, lambda_init)

        h = _out_ple(h, mix_a.reshape(b * s, gw), mix_b.reshape(b * s, gw),
                     p[i].reshape(b * s, -1), w_out[i].astype(BF16), ple_proj[i].astype(BF16),
                     row(ple_gate_norm[i]), ple_gate[i].astype(BF16), 512)
    return h.reshape(b, s, d)
```

```python
import functools
import math

import jax
import jax.numpy as jnp
from jax import lax
from jax.experimental import pallas as pl
from jax.experimental.pallas import tpu as pltpu

F32 = jnp.float32
BF16 = jnp.bfloat16

HEAD_DIM = 128
N_HEADS = 8
DIFF_QK = HEAD_DIM // 2
GROUP_WIDTH = N_HEADS * HEAD_DIM
N_BUCKETS = 32
MAX_DISTANCE = 128
EPS = 1e-6
LOG2E = math.log2(math.e)
MASKED = -3e38
M_FLOOR = -(2.0 ** 100)
Q_PER_K = 4
QUERY_CHUNK = 256
PROJ_CHUNK = 2 * HEAD_DIM
OUT_CHUNK = 512
NORM_ROWS = 256
LANES = 128
AUG = 2 * HEAD_DIM
V_ROWS = HEAD_DIM + 16
VMEM_LIMIT = 56 * 1024 * 1024


def _cparams(sem):
    return pltpu.CompilerParams(dimension_semantics=sem, vmem_limit_bytes=VMEM_LIMIT)


_NT = (((1,), (1,)), ((), ()))


def _split3(v):
    p0 = v.astype(BF16).astype(F32)
    r1 = v - p0
    p1 = r1.astype(BF16).astype(F32)
    p2 = (r1 - p1).astype(BF16).astype(F32)
    return p0, p1, p2


def _norm_forget_kernel(x_ref, g_ref, wf_ref, bf_ref, u_ref, c_ref, carry_ref, *, tiles_per_seq):
    i = pl.program_id(0)

    @pl.when(i % tiles_per_seq == 0)
    def _():
        carry_ref[...] = jnp.zeros_like(carry_ref)

    sub = NORM_ROWS
    row = lax.broadcasted_iota(jnp.int32, (sub, sub), 0)
    col = lax.broadcasted_iota(jnp.int32, (sub, sub), 1)
    tri = jnp.where(col <= row, 1.0, 0.0).astype(BF16)
    wf = wf_ref[...].astype(BF16)
    carry = carry_ref[...]
    for r in range(x_ref.shape[0] // sub):
        rows = slice(r * sub, (r + 1) * sub)
        x = x_ref[rows, :]
        ms = jnp.mean(x * x, axis=-1, keepdims=True)
        u = (x * lax.rsqrt(ms + EPS) * g_ref[...]).astype(BF16)
        u_ref[rows, :] = u

        z = lax.dot_general(u, wf, _NT, preferred_element_type=F32) + bf_ref[...]
        lf = (jnp.minimum(z, 0.0) - jnp.log(1.0 + jnp.exp(-jnp.abs(z)))) * LOG2E
        c = carry
        for part in _split3(lf):
            c = c + jnp.dot(tri, part.astype(BF16), preferred_element_type=F32)
        c_ref[rows, :] = c
        carry = c[sub - 1:sub, :]
    carry_ref[...] = carry


def _norm_forget(x2, gain, w_f, b_f, seq, tm):
    m, d = x2.shape
    kern = functools.partial(_norm_forget_kernel, tiles_per_seq=seq // tm)
    return pl.pallas_call(
        kern,
        grid=(m // tm,),
        in_specs=[
            pl.BlockSpec((tm, d), lambda i: (i, 0)),
            pl.BlockSpec((1, d), lambda i: (0, 0)),
            pl.BlockSpec((LANES, d), lambda i: (0, 0)),
            pl.BlockSpec((1, LANES), lambda i: (0, 0)),
        ],
        out_specs=[
            pl.BlockSpec((tm, d), lambda i: (i, 0)),
            pl.BlockSpec((tm, LANES), lambda i: (i, 0)),
        ],
        out_shape=[
            jax.ShapeDtypeStruct((m, d), BF16),
            jax.ShapeDtypeStruct((m, LANES), F32),
        ],
        scratch_shapes=[pltpu.VMEM((1, LANES), F32)],
        compiler_params=_cparams(("arbitrary",)),
        name="norm_forget",
    )(x2, gain, w_f, b_f)


def _proj_chunks(u_ref, w_ref, wb_ref):
    @pl.when(pl.program_id(0) == 0)
    def _():
        wb_ref[...] = w_ref[...].astype(BF16)

    u = u_ref[...]
    for c in range(wb_ref.shape[0] // PROJ_CHUNK):
        cols = slice(c * PROJ_CHUNK, (c + 1) * PROJ_CHUNK)
        yield c * (PROJ_CHUNK // HEAD_DIM), lax.dot_general(u, wb_ref[cols, :], _NT, preferred_element_type=F32)


def _heads_of(h0, y):
    for j in range(PROJ_CHUNK // HEAD_DIM):
        yield h0 + j, y[:, j * HEAD_DIM:(j + 1) * HEAD_DIM]


def _proj_vt_kernel(u_ref, w_ref, o_ref, wb_ref, *, blk):
    tail_row = lax.broadcasted_iota(jnp.int32, (V_ROWS - HEAD_DIM, blk), 0)
    tail = jnp.where(tail_row == 0, 1.0, 0.0).astype(BF16)
    for h0, y in _proj_chunks(u_ref, w_ref, wb_ref):
        for h, yh in _heads_of(h0, y):
            yt = yh.T.astype(BF16)
            for kb in range(yt.shape[1] // blk):
                o_ref[0, h, kb, :HEAD_DIM, :] = yt[:, kb * blk:(kb + 1) * blk]
                o_ref[0, h, kb, HEAD_DIM:, :] = tail


def _proj_silu_kernel(u_ref, w_ref, o_ref, wb_ref):
    for h0, y in _proj_chunks(u_ref, w_ref, wb_ref):
        o_ref[:, h0 * HEAD_DIM:h0 * HEAD_DIM + PROJ_CHUNK] = (y * (1.0 / (1.0 + jnp.exp(-y)))).astype(o_ref.dtype)


def _proj_heads_kernel(u_ref, w_ref, g_ref, o_ref, wb_ref, *, scale):
    gain = g_ref[...] * scale
    for h0, y in _proj_chunks(u_ref, w_ref, wb_ref):
        for h, yh in _heads_of(h0, y):
            ms = jnp.mean(yh * yh, axis=-1, keepdims=True)
            o_ref[0, h, 0] = (yh * lax.rsqrt(ms + EPS) * gain).T.astype(BF16)


def _proj_fox_key_kernel(u_ref, w_ref, g_ref, c_ref, o_ref, wb_ref):
    tm = u_ref.shape[0]
    lane = lax.broadcasted_iota(jnp.int32, (tm, LANES), 1)
    gain = g_ref[...]
    c_all = c_ref[...]
    for h0, y in _proj_chunks(u_ref, w_ref, wb_ref):
        for h, yh in _heads_of(h0, y):
            ms = jnp.mean(yh * yh, axis=-1, keepdims=True)
            o_ref[:, h * AUG:h * AUG + HEAD_DIM] = (yh * lax.rsqrt(ms + EPS) * gain).astype(BF16)
            p0, p1, p2 = _split3(-jnp.broadcast_to(c_all[:, h:h + 1], (tm, LANES)))
            aug = jnp.where(lane == 3, p0, jnp.where(lane == 4, p1, jnp.where(lane == 5, p2, 0.0)))
            aug = jnp.where(lane < 3, 1.0, aug)
            o_ref[:, h * AUG + HEAD_DIM:(h + 1) * AUG] = aug.astype(BF16)


def _proj_diff_kernel(u_ref, w_ref, g_ref, o_ref, wb_ref, *, scale, transposed):
    tm = u_ref.shape[0]
    lane = lax.broadcasted_iota(jnp.int32, (tm, LANES), 1)
    lo = lane < DIFF_QK
    gain = g_ref[...] * scale
    for h0, y in _proj_chunks(u_ref, w_ref, wb_ref):
        for h, yh in _heads_of(h0, y):
            sq = yh * yh
            s_lo = jnp.sum(jnp.where(lo, sq, 0.0), axis=-1, keepdims=True)
            s_all = jnp.sum(sq, axis=-1, keepdims=True)
            ms = jnp.where(lo, s_lo, s_all - s_lo) * (1.0 / DIFF_QK)
            normed = yh * lax.rsqrt(ms + EPS) * gain
            if transposed:
                o_ref[0, h, 0] = normed.T.astype(BF16)
            else:
                o_ref[:, h * HEAD_DIM:(h + 1) * HEAD_DIM] = normed.astype(BF16)


def _proj(kern, u, w_t, layer, section, extra, extra_specs, out_spec, out_shape, tm, name):
    m, d = u.shape
    n = GROUP_WIDTH
    return pl.pallas_call(
        kern,
        grid=(m // tm,),
        in_specs=[
            pl.BlockSpec((tm, d), lambda i: (i, 0)),
            pl.BlockSpec((None, n, d), lambda i: (layer, section, 0), pipeline_mode=pl.Buffered(1)),
        ] + extra_specs,
        out_specs=out_spec,
        out_shape=out_shape,
        scratch_shapes=[pltpu.VMEM((n, d), BF16)],
        compiler_params=_cparams(("arbitrary",)),
        name=name,
    )(u, w_t, *extra)


def _attn_scratch(blk, width):
    return [
        pltpu.VMEM((blk, width), F32),
        pltpu.VMEM((1, width), F32),
        pltpu.VMEM((2, blk, width), BF16),
        pltpu.VMEM((2, 1, width), F32),
        pltpu.VMEM((1, width), F32),
        pltpu.VMEM((V_ROWS, width), F32),
    ]


def _query_chunks(blk, n_maps):
    qw = Q_PER_K * blk
    return [slice(lo, lo + QUERY_CHUNK) for lo in range(0, n_maps * qw, QUERY_CHUNK)]


def _attn_pipeline(qi, scores, vt_of, blk, chunks, s_ref, mx_ref, p_ref, al_ref, m_ref, acc_ref):
    n = Q_PER_K * (qi + 1)
    qw = Q_PER_K * blk

    def key_block(t):
        return jnp.maximum(n - 1 - t, 0)

    def ahead(t, lanes):
        return (lanes.start % qw) // blk - (Q_PER_K - 1 - t)

    def qk(t, lanes, far=False):
        st = scores(key_block(t), 2 if far else min(ahead(t, lanes), 2), lanes)
        s_ref[:, lanes] = st
        mx_ref[:, lanes] = jnp.max(st, axis=0, keepdims=True)

    def softmax(slot, lanes):
        m_old = m_ref[:, lanes]
        m_new = jnp.maximum(m_old, mx_ref[:, lanes])
        m_ref[:, lanes] = m_new
        al_ref[slot, :, lanes] = jnp.exp2(m_old - m_new)
        p_ref[slot, :, lanes] = jnp.exp2(s_ref[:, lanes] - m_new).astype(BF16)

    def pv(t, slot, lanes):
        acc_ref[:, lanes] = al_ref[slot, :, lanes] * acc_ref[:, lanes] + jnp.dot(
            vt_of(key_block(t)), p_ref[slot, :, lanes], preferred_element_type=F32)

    def step(t, slot, with_qk=True, static_t=None):
        for lanes in chunks:
            if static_t is None or ahead(static_t, lanes) >= 0:
                softmax(slot, lanes)
            if static_t is None or ahead(static_t - 1, lanes) >= 0:
                pv(t - 1, 1 - slot, lanes)
            if with_qk and (static_t is None or ahead(static_t + 1, lanes) >= 0):
                qk(t + 1, lanes, far=static_t is None)

    m_ref[...] = jnp.full(m_ref.shape, M_FLOOR, F32)
    acc_ref[...] = jnp.zeros(acc_ref.shape, F32)

    for lanes in chunks:
        if ahead(0, lanes) >= 0:
            qk(0, lanes)
    for t in range(Q_PER_K):
        step(t, t % 2, static_t=t)

    assert Q_PER_K % 2 == 0
    @pl.loop(0, jnp.maximum(Q_PER_K // 2 * qi - 1, 0))
    def _(u):
        step(2 * u + Q_PER_K, 0)
        step(2 * u + Q_PER_K + 1, 1)

    @pl.when(qi >= 1)
    def _():
        step(n - 2, 0)
        step(n - 1, 1, with_qk=False)

    for lanes in chunks:
        pv(n - 1, 1, lanes)


def _fox_attn_kernel(q_ref, c_ref, k_ref, vt_ref, gate_ref, o_ref, *scratch, blk):
    qw = Q_PER_K * blk
    key = lax.broadcasted_iota(jnp.int32, (blk, blk), 0)
    qry = lax.broadcasted_iota(jnp.int32, (blk, blk), 1)
    causal = key <= qry
    dim = lax.broadcasted_iota(jnp.int32, (AUG - HEAD_DIM, qw), 0)
    chunks = _query_chunks(blk, 1)

    @pl.loop(0, q_ref.shape[2])
    def _(qi):
        p0, p1, p2 = _split3(c_ref[0, 0, qi])
        decay = jnp.where(dim == 0, p0, jnp.where(dim == 1, p1, jnp.where(dim == 2, p2, 0.0)))
        decay = jnp.where((dim >= 3) & (dim < 6), 1.0, decay)
        qt = jnp.concatenate([q_ref[0, 0, qi], decay.astype(BF16)], axis=0)

        def scores(kb, ahead, lanes):
            k = k_ref[0, pl.ds(pl.multiple_of(kb * blk, blk), blk), :]
            st = jnp.dot(k, qt[:, lanes], preferred_element_type=F32)
            if ahead == 0:
                within = slice(lanes.start % blk, lanes.start % blk + QUERY_CHUNK)
                return jnp.where(causal[:, within], st, MASKED)
            return st

        _attn_pipeline(qi, scores, lambda kb: vt_ref[0, 0, kb], blk, chunks, *scratch)
        acc = scratch[-1][...]
        ot = acc[:HEAD_DIM] / acc[HEAD_DIM:HEAD_DIM + 1]
        rows = pl.ds(pl.multiple_of(qi * qw, qw), qw)
        o_ref[0, rows, :] = (ot.T * gate_ref[0, rows, :]).astype(o_ref.dtype)


def _fox_attn(qt, c_rows, k, vt, gate, blk):
    b, s, _ = k.shape
    qw = Q_PER_K * blk
    head = lambda *block: pl.BlockSpec((1, 1) + block, lambda bi, h: (bi, h) + (0,) * len(block))
    cols = lambda width: pl.BlockSpec((1, s, width), lambda bi, h: (bi, 0, h))
    return pl.pallas_call(
        functools.partial(_fox_attn_kernel, blk=blk),
        grid=(b, N_HEADS),
        in_specs=[
            head(s // qw, HEAD_DIM, qw),
            head(s // qw, 1, qw),
            cols(AUG),
            head(s // blk, V_ROWS, blk),
            cols(HEAD_DIM),
        ],
        out_specs=cols(HEAD_DIM),
        out_shape=jax.ShapeDtypeStruct((b, s, GROUP_WIDTH), BF16),
        scratch_shapes=_attn_scratch(blk, qw),
        compiler_params=_cparams(("arbitrary", "arbitrary")),
        name="fox_attn",
    )(qt, c_rows, k, vt, gate)


def _diff_attn_kernel(q_ref, k_ref, vt_ref, gate_ref, table_ref, lam_ref, onorm_ref, o_ref, bias_ref,
                      *scratch, blk, lambda_init):
    qw = Q_PER_K * blk
    shifted = pltpu.roll(jnp.broadcast_to(table_ref[0], (blk, 2 * blk)), 0, 1, stride=1, stride_axis=0)
    key = lax.broadcasted_iota(jnp.int32, (blk, blk), 0)
    qry = lax.broadcasted_iota(jnp.int32, (blk, blk), 1)
    bias_ref[0] = jnp.where(key <= qry, shifted[:, :blk], MASKED)
    bias_ref[1] = shifted[:, blk:]

    lam_p = lam_ref[...]
    lam = (jnp.exp(jnp.sum(lam_p[0:1] * lam_p[1:2], axis=-1, keepdims=True))
           - jnp.exp(jnp.sum(lam_p[2:3] * lam_p[3:4], axis=-1, keepdims=True)) + lambda_init)
    zeros = jnp.zeros((DIFF_QK, qw), BF16)
    chunks = _query_chunks(blk, 2)

    @pl.loop(0, q_ref.shape[2])
    def _(qi):
        qt = q_ref[0, 0, qi]
        qzt = jnp.concatenate([jnp.concatenate([qt[:DIFF_QK], zeros], axis=0),
                               jnp.concatenate([zeros, qt[DIFF_QK:]], axis=0)], axis=1)

        def scores(kb, ahead, lanes):
            k = k_ref[0, pl.ds(pl.multiple_of(kb * blk, blk), blk), :]
            st = jnp.dot(k, qzt[:, lanes], preferred_element_type=F32)
            if ahead < 2:
                within = slice(lanes.start % blk, lanes.start % blk + QUERY_CHUNK)
                return st + bias_ref[ahead, :, within]
            return st

        _attn_pipeline(qi, scores, lambda kb: vt_ref[0, 0, kb], blk, chunks, *scratch)

        acc = scratch[-1][...]
        ot = acc[:HEAD_DIM] / acc[HEAD_DIM:HEAD_DIM + 1]
        o = (ot[:, :qw] - lam * ot[:, qw:]).T
        ms = jnp.mean(o * o, axis=-1, keepdims=True)
        o = o * lax.rsqrt(ms + EPS) * onorm_ref[...] * (1.0 - lambda_init)
        rows = pl.ds(pl.multiple_of(qi * qw, qw), qw)
        o_ref[0, rows, :] = (o * gate_ref[0, rows, :]).astype(o_ref.dtype)


def _diff_attn(qt, k, vt, gate, bias_table, lam_params, onorm, blk, lambda_init):
    b, s, _ = k.shape
    qw = Q_PER_K * blk
    head = lambda *block: pl.BlockSpec((1, 1) + block, lambda bi, h: (bi, h) + (0,) * len(block))
    cols = lambda width: pl.BlockSpec((1, s, width), lambda bi, h: (bi, 0, h))
    return pl.pallas_call(
        functools.partial(_diff_attn_kernel, blk=blk, lambda_init=lambda_init),
        grid=(b, N_HEADS),
        in_specs=[
            head(s // qw, HEAD_DIM, qw),
            cols(HEAD_DIM),
            head(s // blk, V_ROWS, blk),
            cols(HEAD_DIM),
            pl.BlockSpec((1, 1, 2 * blk), lambda bi, h: (h, 0, 0)),
            pl.BlockSpec((4, DIFF_QK), lambda bi, h: (0, 0)),
            pl.BlockSpec((1, HEAD_DIM), lambda bi, h: (0, 0)),
        ],
        out_specs=cols(HEAD_DIM),
        out_shape=jax.ShapeDtypeStruct((b, s, GROUP_WIDTH), BF16),
        scratch_shapes=[pltpu.VMEM((2, blk, blk), F32)] + _attn_scratch(blk, 2 * qw),
        compiler_params=_cparams(("arbitrary", "arbitrary")),
        name="diff_attn",
    )(qt, k, vt, gate, bias_table, lam_params, onorm)


def _out_ple_kernel(x_ref, ma_ref, mb_ref, p_ref, wo_ref, wp_ref, gn_ref, wg_ref, o_ref, h_ref):
    half = ma_ref.shape[1]
    d = o_ref.shape[1]
    chunks = [slice(lo, lo + OUT_CHUNK) for lo in range(0, d, OUT_CHUNK)]

    ma, mb = ma_ref[...], mb_ref[...]
    sum_sq = jnp.zeros((ma.shape[0], 1), F32)
    for cols in chunks:
        hc = (x_ref[:, cols] + jnp.dot(ma, wo_ref[:half, cols], preferred_element_type=F32)
              + jnp.dot(mb, wo_ref[half:, cols], preferred_element_type=F32))
        h_ref[:, cols] = hc
        sum_sq = sum_sq + jnp.sum(hc * hc, axis=-1, keepdims=True)

    hn = (h_ref[...] * lax.rsqrt(sum_sq * (1.0 / d) + EPS) * gn_ref[...]).astype(BF16)
    pb = p_ref[...].astype(BF16)
    for cols in chunks:
        zg = jnp.dot(hn, wg_ref[:, cols], preferred_element_type=F32)
        e = jnp.dot(pb, wp_ref[:, cols], preferred_element_type=F32)
        o_ref[:, cols] = h_ref[:, cols] + e * (1.0 / (1.0 + jnp.exp(-zg)))


def _out_ple(x2, mix_a, mix_b, p2, w_out, ple_proj, gate_norm, ple_gate, tm):
    m, d = x2.shape
    half = mix_a.shape[1]
    pdim = p2.shape[1]
    const = lambda i: (0, 0)
    single = dict(pipeline_mode=pl.Buffered(1))
    return pl.pallas_call(
        _out_ple_kernel,
        grid=(m // tm,),
        in_specs=[
            pl.BlockSpec((tm, d), lambda i: (i, 0)),
            pl.BlockSpec((tm, half), lambda i: (i, 0)),
            pl.BlockSpec((tm, half), lambda i: (i, 0)),
            pl.BlockSpec((tm, pdim), lambda i: (i, 0)),
            pl.BlockSpec((2 * half, d), const, **single),
            pl.BlockSpec((pdim, d), const, **single),
            pl.BlockSpec((1, d), const),
            pl.BlockSpec((d, d), const, **single),
        ],
        out_specs=pl.BlockSpec((tm, d), lambda i: (i, 0)),
        out_shape=jax.ShapeDtypeStruct((m, d), F32),
        scratch_shapes=[pltpu.VMEM((tm, d), F32)],
        compiler_params=_cparams(("arbitrary",)),
        name="out_ple",
    )(x2, mix_a, mix_b, p2, w_out, ple_proj, gate_norm, ple_gate)


def _t5_bucket(n):
    max_exact = N_BUCKETS // 2
    nf = jnp.maximum(n, 1).astype(F32)
    large = max_exact + (jnp.log(nf / max_exact) / math.log(MAX_DISTANCE / max_exact)
                         * (N_BUCKETS - max_exact)).astype(jnp.int32)
    large = jnp.minimum(large, N_BUCKETS - 1)
    return jnp.where(n < max_exact, n, large)


def _bias_table(rel_bias, blk):
    assert blk >= MAX_DISTANCE
    table = rel_bias.astype(F32)[_t5_bucket(jnp.arange(2 * blk))].T
    table = (table - rel_bias.astype(F32)[N_BUCKETS - 1][:, None]) * LOG2E
    return table[:, None, :]


def kernel(x, p, attn_norm, w_in, b_forget, fox_q_norm, fox_k_norm, diff_q_norm, diff_k_norm,
           lambda_q1, lambda_k1, lambda_q2, lambda_k2, diff_out_norm, w_out, rel_bias,
           ple_proj, ple_gate_norm, ple_gate):
    b, s, d = x.shape
    depth = w_in.shape[0]
    gw = GROUP_WIDTH
    blk = 512
    tm = 1024
    bias_table = _bias_table(rel_bias, blk)
    w_t = jnp.swapaxes(w_in.astype(F32), 1, 2)

    h = x.reshape(b * s, d)
    for i in range(depth):
        lambda_init = 0.8 - 0.6 * math.exp(-0.3 * i)
        w_f = jnp.pad(w_t[i, 8 * gw:, :], ((0, LANES - N_HEADS), (0, 0)))
        b_f = jnp.pad(b_forget[i].astype(F32), (0, LANES - N_HEADS)).reshape(1, LANES)
        u, c2 = _norm_forget(h, attn_norm[i].astype(F32).reshape(1, d), w_f, b_f, s, tm)

        row = lambda a: a.astype(F32).reshape(1, -1)
        vec_spec = lambda n: pl.BlockSpec((1, n), lambda r: (0, 0))
        c_spec = pl.BlockSpec((tm, LANES), lambda r: (r, 0))
        fox_scale = HEAD_DIM ** -0.5 * LOG2E
        diff_scale = DIFF_QK ** -0.5 * LOG2E

        tiles = s // tm
        flat = lambda width, dtype: (pl.BlockSpec((tm, width), lambda r: (r, 0)),
                                     jax.ShapeDtypeStruct((b * s, width), dtype))
        qw = Q_PER_K * blk
        per_q = qw // tm
        heads_t = (pl.BlockSpec((1, N_HEADS, 1, HEAD_DIM, tm),
                                lambda r: (r // tiles, 0, (r % tiles) // per_q, 0, (r % tiles) % per_q)),
                   jax.ShapeDtypeStruct((b, N_HEADS, s // qw, HEAD_DIM, qw), BF16))
        v_t = (pl.BlockSpec((1, N_HEADS, tm // blk, V_ROWS, blk), lambda r: (r // tiles, 0, r % tiles, 0, 0)),
               jax.ShapeDtypeStruct((b, N_HEADS, s // blk, V_ROWS, blk), BF16))
        proj = lambda kern, section, extra, specs, out, name: _proj(
            kern, u, w_t, i, section, extra, specs, out[0], out[1], tm, name)
        vt_kernel = functools.partial(_proj_vt_kernel, blk=blk)

        qa = proj(functools.partial(_proj_heads_kernel, scale=fox_scale), 0,
                  [row(fox_q_norm[i])], [vec_spec(HEAD_DIM)], heads_t, "proj_fox_q")
        ka = proj(_proj_fox_key_kernel, 1, [row(fox_k_norm[i]), c2], [vec_spec(HEAD_DIM), c_spec],
                  flat(N_HEADS * AUG, BF16), "proj_fox_k")
        va = proj(vt_kernel, 2, [], [], v_t, "proj_fox_v")
        ga = proj(_proj_silu_kernel, 3, [], [], flat(gw, F32), "proj_fox_gate")
        dq_gain = row(jnp.tile(diff_q_norm[i], 2))
        dk_gain = row(jnp.tile(diff_k_norm[i], 2))
        qb = proj(functools.partial(_proj_diff_kernel, scale=diff_scale, transposed=True), 4,
                  [dq_gain], [vec_spec(HEAD_DIM)], heads_t, "proj_diff_q")
        kb = proj(functools.partial(_proj_diff_kernel, scale=1.0, transposed=False), 5,
                  [dk_gain], [vec_spec(HEAD_DIM)], flat(gw, BF16), "proj_diff_k")
        vb = proj(vt_kernel, 6, [], [], v_t, "proj_diff_v")
        gb = proj(_proj_silu_kernel, 7, [], [], flat(gw, F32), "proj_diff_gate")

        to3 = lambda a: a.reshape(b, s, a.shape[-1])
        c_rows = (c2[:, :N_HEADS].reshape(b, s // qw, qw, N_HEADS).transpose(0, 3, 1, 2)
                  .reshape(b, N_HEADS, s // qw, 1, qw))
        mix_a = _fox_attn(qa, c_rows, to3(ka), va, to3(ga), blk)
        lam_params = jnp.stack([lambda_q1[i], lambda_k1[i], lambda_q2[i], lambda_k2[i]]).astype(F32)
        mix_b = _diff_attn(qb, to3(kb), vb, to3(gb), bias_table, lam_params,
                           row(diff_out_norm[i]), blk, lambda_init)

        h = _out_ple(h, mix_a.reshape(b * s, gw), mix_b.reshape(b * s, gw),
                     p[i].reshape(b * s, -1), w_out[i].astype(BF16), ple_proj[i].astype(BF16),
                     row(ple_gate_norm[i]), ple_gate[i].astype(BF16), 512)
    return h.reshape(b, s, d)
```

```python
import functools
import math

import jax
import jax.numpy as jnp
from jax import lax
from jax.experimental import pallas as pl
from jax.experimental.pallas import tpu as pltpu

F32 = jnp.float32
BF16 = jnp.bfloat16

HEAD_DIM = 128
N_HEADS = 8
DIFF_QK = HEAD_DIM // 2
GROUP_WIDTH = N_HEADS * HEAD_DIM
N_BUCKETS = 32
MAX_DISTANCE = 128
EPS = 1e-6
LOG2E = math.log2(math.e)
MASKED = -3e38
M_FLOOR = -(2.0 ** 100)
LANES = 128
AUG = 2 * HEAD_DIM
V_ROWS = HEAD_DIM + 16

KEY_BLOCK = 512
Q_PER_K = 4
QUERY_CHUNK = 256
PROJ_ROWS = 1024
PROJ_CHUNK = 2 * HEAD_DIM
NORM_ROWS = 256
OUT_ROWS = 512
OUT_CHUNK = 512
VMEM_LIMIT = 56 * 1024 * 1024


def _cparams(sem):
    return pltpu.CompilerParams(dimension_semantics=sem, vmem_limit_bytes=VMEM_LIMIT)


_NT = (((1,), (1,)), ((), ()))


def _split3(v):
    p0 = v.astype(BF16).astype(F32)
    r1 = v - p0
    p1 = r1.astype(BF16).astype(F32)
    p2 = (r1 - p1).astype(BF16).astype(F32)
    return p0, p1, p2


def _norm_forget_kernel(x_ref, g_ref, wf_ref, bf_ref, u_ref, c_ref, carry_ref, *, tiles_per_seq):
    i = pl.program_id(0)

    @pl.when(i % tiles_per_seq == 0)
    def _():
        carry_ref[...] = jnp.zeros_like(carry_ref)

    sub = NORM_ROWS
    row = lax.broadcasted_iota(jnp.int32, (sub, sub), 0)
    col = lax.broadcasted_iota(jnp.int32, (sub, sub), 1)
    tri = jnp.where(col <= row, 1.0, 0.0).astype(BF16)
    wf = wf_ref[...].astype(BF16)
    carry = carry_ref[...]
    for r in range(x_ref.shape[0] // sub):
        rows = slice(r * sub, (r + 1) * sub)
        x = x_ref[rows, :]
        ms = jnp.mean(x * x, axis=-1, keepdims=True)
        u = (x * lax.rsqrt(ms + EPS) * g_ref[...]).astype(BF16)
        u_ref[rows, :] = u

        z = lax.dot_general(u, wf, _NT, preferred_element_type=F32) + bf_ref[...]
        lf = (jnp.minimum(z, 0.0) - jnp.log(1.0 + jnp.exp(-jnp.abs(z)))) * LOG2E
        c = carry
        for part in _split3(lf):
            c = c + jnp.dot(tri, part.astype(BF16), preferred_element_type=F32)
        c_ref[rows, :] = c
        carry = c[sub - 1:sub, :]
    carry_ref[...] = carry


def _norm_forget(x2, gain, w_f, b_f, seq, tm):
    m, d = x2.shape
    kern = functools.partial(_norm_forget_kernel, tiles_per_seq=seq // tm)
    return pl.pallas_call(
        kern,
        grid=(m // tm,),
        in_specs=[
            pl.BlockSpec((tm, d), lambda i: (i, 0)),
            pl.BlockSpec((1, d), lambda i: (0, 0)),
            pl.BlockSpec((LANES, d), lambda i: (0, 0)),
            pl.BlockSpec((1, LANES), lambda i: (0, 0)),
        ],
        out_specs=[
            pl.BlockSpec((tm, d), lambda i: (i, 0)),
            pl.BlockSpec((tm, LANES), lambda i: (i, 0)),
        ],
        out_shape=[
            jax.ShapeDtypeStruct((m, d), BF16),
            jax.ShapeDtypeStruct((m, LANES), F32),
        ],
        scratch_shapes=[pltpu.VMEM((1, LANES), F32)],
        compiler_params=_cparams(("arbitrary",)),
        name="norm_forget",
    )(x2, gain, w_f, b_f)


def _proj_chunks(u_ref, w_ref, wb_ref):
    @pl.when(pl.program_id(0) == 0)
    def _():
        wb_ref[...] = w_ref[...].astype(BF16)

    u = u_ref[...]
    for c in range(wb_ref.shape[0] // PROJ_CHUNK):
        cols = slice(c * PROJ_CHUNK, (c + 1) * PROJ_CHUNK)
        yield c * (PROJ_CHUNK // HEAD_DIM), lax.dot_general(u, wb_ref[cols, :], _NT, preferred_element_type=F32)


def _heads_of(h0, y):
    for j in range(PROJ_CHUNK // HEAD_DIM):
        yield h0 + j, y[:, j * HEAD_DIM:(j + 1) * HEAD_DIM]


def _proj_vt_kernel(u_ref, w_ref, o_ref, wb_ref, *, blk):
    tail_row = lax.broadcasted_iota(jnp.int32, (V_ROWS - HEAD_DIM, blk), 0)
    tail = jnp.where(tail_row == 0, 1.0, 0.0).astype(BF16)
    for h0, y in _proj_chunks(u_ref, w_ref, wb_ref):
        for h, yh in _heads_of(h0, y):
            yt = yh.T.astype(BF16)
            for kb in range(yt.shape[1] // blk):
                o_ref[0, h, kb, :HEAD_DIM, :] = yt[:, kb * blk:(kb + 1) * blk]
                o_ref[0, h, kb, HEAD_DIM:, :] = tail


def _proj_silu_kernel(u_ref, w_ref, o_ref, wb_ref):
    for h0, y in _proj_chunks(u_ref, w_ref, wb_ref):
        o_ref[:, h0 * HEAD_DIM:h0 * HEAD_DIM + PROJ_CHUNK] = (y * (1.0 / (1.0 + jnp.exp(-y)))).astype(o_ref.dtype)


def _proj_heads_kernel(u_ref, w_ref, g_ref, o_ref, wb_ref, *, scale):
    gain = g_ref[...] * scale
    for h0, y in _proj_chunks(u_ref, w_ref, wb_ref):
        for h, yh in _heads_of(h0, y):
            ms = jnp.mean(yh * yh, axis=-1, keepdims=True)
            o_ref[0, h, 0] = (yh * lax.rsqrt(ms + EPS) * gain).T.astype(BF16)


def _proj_fox_key_kernel(u_ref, w_ref, g_ref, c_ref, o_ref, wb_ref):
    tm = u_ref.shape[0]
    lane = lax.broadcasted_iota(jnp.int32, (tm, LANES), 1)
    gain = g_ref[...]
    c_all = c_ref[...]
    for h0, y in _proj_chunks(u_ref, w_ref, wb_ref):
        for h, yh in _heads_of(h0, y):
            ms = jnp.mean(yh * yh, axis=-1, keepdims=True)
            o_ref[:, h * AUG:h * AUG + HEAD_DIM] = (yh * lax.rsqrt(ms + EPS) * gain).astype(BF16)
            p0, p1, p2 = _split3(-jnp.broadcast_to(c_all[:, h:h + 1], (tm, LANES)))
            aug = jnp.where(lane == 3, p0, jnp.where(lane == 4, p1, jnp.where(lane == 5, p2, 0.0)))
            aug = jnp.where(lane < 3, 1.0, aug)
            o_ref[:, h * AUG + HEAD_DIM:(h + 1) * AUG] = aug.astype(BF16)


def _proj_diff_kernel(u_ref, w_ref, g_ref, o_ref, wb_ref, *, scale, transposed):
    tm = u_ref.shape[0]
    lane = lax.broadcasted_iota(jnp.int32, (tm, LANES), 1)
    lo = lane < DIFF_QK
    gain = g_ref[...] * scale
    for h0, y in _proj_chunks(u_ref, w_ref, wb_ref):
        for h, yh in _heads_of(h0, y):
            sq = yh * yh
            s_lo = jnp.sum(jnp.where(lo, sq, 0.0), axis=-1, keepdims=True)
            s_all = jnp.sum(sq, axis=-1, keepdims=True)
            ms = jnp.where(lo, s_lo, s_all - s_lo) * (1.0 / DIFF_QK)
            normed = yh * lax.rsqrt(ms + EPS) * gain
            if transposed:
                o_ref[0, h, 0] = normed.T.astype(BF16)
            else:
                o_ref[:, h * HEAD_DIM:(h + 1) * HEAD_DIM] = normed.astype(BF16)


def _proj(kern, u, w_t, layer, section, extra, extra_specs, out_spec, out_shape, tm, name):
    m, d = u.shape
    n = GROUP_WIDTH
    return pl.pallas_call(
        kern,
        grid=(m // tm,),
        in_specs=[
            pl.BlockSpec((tm, d), lambda i: (i, 0)),
            pl.BlockSpec((None, n, d), lambda i: (layer, section, 0), pipeline_mode=pl.Buffered(1)),
        ] + extra_specs,
        out_specs=out_spec,
        out_shape=out_shape,
        scratch_shapes=[pltpu.VMEM((n, d), BF16)],
        compiler_params=_cparams(("arbitrary",)),
        name=name,
    )(u, w_t, *extra)


def _attn_scratch(blk, width):
    return [
        pltpu.VMEM((blk, width), F32),
        pltpu.VMEM((1, width), F32),
        pltpu.VMEM((2, blk, width), BF16),
        pltpu.VMEM((2, 1, width), F32),
        pltpu.VMEM((1, width), F32),
        pltpu.VMEM((V_ROWS, width), F32),
    ]


def _query_chunks(blk, n_maps):
    qw = Q_PER_K * blk
    return [slice(lo, lo + QUERY_CHUNK) for lo in range(0, n_maps * qw, QUERY_CHUNK)]


def _attn_pipeline(qi, scores, vt_of, blk, chunks, s_ref, mx_ref, p_ref, al_ref, m_ref, acc_ref):
    n = Q_PER_K * (qi + 1)
    qw = Q_PER_K * blk

    def key_block(t):
        return jnp.maximum(n - 1 - t, 0)

    def ahead(t, lanes):
        return (lanes.start % qw) // blk - (Q_PER_K - 1 - t)

    def qk(t, lanes, far=False):
        st = scores(key_block(t), 2 if far else min(ahead(t, lanes), 2), lanes)
        s_ref[:, lanes] = st
        mx_ref[:, lanes] = jnp.max(st, axis=0, keepdims=True)

    def softmax(slot, lanes):
        m_old = m_ref[:, lanes]
        m_new = jnp.maximum(m_old, mx_ref[:, lanes])
        m_ref[:, lanes] = m_new
        al_ref[slot, :, lanes] = jnp.exp2(m_old - m_new)
        p_ref[slot, :, lanes] = jnp.exp2(s_ref[:, lanes] - m_new).astype(BF16)

    def pv(t, slot, lanes):
        acc_ref[:, lanes] = al_ref[slot, :, lanes] * acc_ref[:, lanes] + jnp.dot(
            vt_of(key_block(t)), p_ref[slot, :, lanes], preferred_element_type=F32)

    def step(t, slot, with_qk=True, static_t=None):
        for lanes in chunks:
            if static_t is None or ahead(static_t, lanes) >= 0:
                softmax(slot, lanes)
            if static_t is None or ahead(static_t - 1, lanes) >= 0:
                pv(t - 1, 1 - slot, lanes)
            if with_qk and (static_t is None or ahead(static_t + 1, lanes) >= 0):
                qk(t + 1, lanes, far=static_t is None)

    m_ref[...] = jnp.full(m_ref.shape, M_FLOOR, F32)
    acc_ref[...] = jnp.zeros(acc_ref.shape, F32)

    for lanes in chunks:
        if ahead(0, lanes) >= 0:
            qk(0, lanes)
    for t in range(Q_PER_K):
        step(t, t % 2, static_t=t)

    assert Q_PER_K % 2 == 0
    @pl.loop(0, jnp.maximum(Q_PER_K // 2 * qi - 1, 0))
    def _(u):
        step(2 * u + Q_PER_K, 0)
        step(2 * u + Q_PER_K + 1, 1)

    @pl.when(qi >= 1)
    def _():
        step(n - 2, 0)
        step(n - 1, 1, with_qk=False)

    for lanes in chunks:
        pv(n - 1, 1, lanes)


def _fox_attn_kernel(q_ref, c_ref, k_ref, vt_ref, gate_ref, o_ref, *scratch, blk):
    qw = Q_PER_K * blk
    key = lax.broadcasted_iota(jnp.int32, (blk, blk), 0)
    qry = lax.broadcasted_iota(jnp.int32, (blk, blk), 1)
    causal = key <= qry
    dim = lax.broadcasted_iota(jnp.int32, (AUG - HEAD_DIM, qw), 0)
    chunks = _query_chunks(blk, 1)

    @pl.loop(0, q_ref.shape[2])
    def _(qi):
        p0, p1, p2 = _split3(c_ref[0, 0, qi])
        decay = jnp.where(dim == 0, p0, jnp.where(dim == 1, p1, jnp.where(dim == 2, p2, 0.0)))
        decay = jnp.where((dim >= 3) & (dim < 6), 1.0, decay)
        qt = jnp.concatenate([q_ref[0, 0, qi], decay.astype(BF16)], axis=0)

        def scores(kb, ahead, lanes):
            k = k_ref[0, pl.ds(pl.multiple_of(kb * blk, blk), blk), :]
            st = jnp.dot(k, qt[:, lanes], preferred_element_type=F32)
            if ahead == 0:
                within = slice(lanes.start % blk, lanes.start % blk + QUERY_CHUNK)
                return jnp.where(causal[:, within], st, MASKED)
            return st

        _attn_pipeline(qi, scores, lambda kb: vt_ref[0, 0, kb], blk, chunks, *scratch)
        acc = scratch[-1][...]
        ot = acc[:HEAD_DIM] / acc[HEAD_DIM:HEAD_DIM + 1]
        rows = pl.ds(pl.multiple_of(qi * qw, qw), qw)
        o_ref[0, rows, :] = (ot.T * gate_ref[0, rows, :]).astype(o_ref.dtype)


def _fox_attn(qt, c_rows, k, vt, gate, blk):
    b, s, _ = k.shape
    qw = Q_PER_K * blk
    head = lambda *block: pl.BlockSpec((1, 1) + block, lambda bi, h: (bi, h) + (0,) * len(block))
    cols = lambda width: pl.BlockSpec((1, s, width), lambda bi, h: (bi, 0, h))
    return pl.pallas_call(
        functools.partial(_fox_attn_kernel, blk=blk),
        grid=(b, N_HEADS),
        in_specs=[
            head(s // qw, HEAD_DIM, qw),
            head(s // qw, 1, qw),
            cols(AUG),
            head(s // blk, V_ROWS, blk),
            cols(HEAD_DIM),
        ],
        out_specs=cols(HEAD_DIM),
        out_shape=jax.ShapeDtypeStruct((b, s, GROUP_WIDTH), BF16),
        scratch_shapes=_attn_scratch(blk, qw),
        compiler_params=_cparams(("arbitrary", "arbitrary")),
        name="fox_attn",
    )(qt, c_rows, k, vt, gate)


def _diff_attn_kernel(q_ref, k_ref, vt_ref, gate_ref, table_ref, lam_ref, onorm_ref, o_ref, bias_ref,
                      *scratch, blk, lambda_init):
    qw = Q_PER_K * blk
    shifted = pltpu.roll(jnp.broadcast_to(table_ref[0], (blk, 2 * blk)), 0, 1, stride=1, stride_axis=0)
    key = lax.broadcasted_iota(jnp.int32, (blk, blk), 0)
    qry = lax.broadcasted_iota(jnp.int32, (blk, blk), 1)
    bias_ref[0] = jnp.where(key <= qry, shifted[:, :blk], MASKED)
    bias_ref[1] = shifted[:, blk:]

    lam_p = lam_ref[...]
    lam = (jnp.exp(jnp.sum(lam_p[0:1] * lam_p[1:2], axis=-1, keepdims=True))
           - jnp.exp(jnp.sum(lam_p[2:3] * lam_p[3:4], axis=-1, keepdims=True)) + lambda_init)
    zeros = jnp.zeros((DIFF_QK, qw), BF16)
    chunks = _query_chunks(blk, 2)

    @pl.loop(0, q_ref.shape[2])
    def _(qi):
        qt = q_ref[0, 0, qi]
        qzt = jnp.concatenate([jnp.concatenate([qt[:DIFF_QK], zeros], axis=0),
                               jnp.concatenate([zeros, qt[DIFF_QK:]], axis=0)], axis=1)

        def scores(kb, ahead, lanes):
            k = k_ref[0, pl.ds(pl.multiple_of(kb * blk, blk), blk), :]
            st = jnp.dot(k, qzt[:, lanes], preferred_element_type=F32)
            if ahead < 2:
                within = slice(lanes.start % blk, lanes.start % blk + QUERY_CHUNK)
                return st + bias_ref[ahead, :, within]
            return st

        _attn_pipeline(qi, scores, lambda kb: vt_ref[0, 0, kb], blk, chunks, *scratch)

        acc = scratch[-1][...]
        ot = acc[:HEAD_DIM] / acc[HEAD_DIM:HEAD_DIM + 1]
        o = (ot[:, :qw] - lam * ot[:, qw:]).T
        ms = jnp.mean(o * o, axis=-1, keepdims=True)
        o = o * lax.rsqrt(ms + EPS) * onorm_ref[...] * (1.0 - lambda_init)
        rows = pl.ds(pl.multiple_of(qi * qw, qw), qw)
        o_ref[0, rows, :] = (o * gate_ref[0, rows, :]).astype(o_ref.dtype)


def _diff_attn(qt, k, vt, gate, bias_table, lam_params, onorm, blk, lambda_init):
    b, s, _ = k.shape
    qw = Q_PER_K * blk
    head = lambda *block: pl.BlockSpec((1, 1) + block, lambda bi, h: (bi, h) + (0,) * len(block))
    cols = lambda width: pl.BlockSpec((1, s, width), lambda bi, h: (bi, 0, h))
    return pl.pallas_call(
        functools.partial(_diff_attn_kernel, blk=blk, lambda_init=lambda_init),
        grid=(b, N_HEADS),
        in_specs=[
            head(s // qw, HEAD_DIM, qw),
            cols(HEAD_DIM),
            head(s // blk, V_ROWS, blk),
            cols(HEAD_DIM),
            pl.BlockSpec((1, 1, 2 * blk), lambda bi, h: (h, 0, 0)),
            pl.BlockSpec((4, DIFF_QK), lambda bi, h: (0, 0)),
            pl.BlockSpec((1, HEAD_DIM), lambda bi, h: (0, 0)),
        ],
        out_specs=cols(HEAD_DIM),
        out_shape=jax.ShapeDtypeStruct((b, s, GROUP_WIDTH), BF16),
        scratch_shapes=[pltpu.VMEM((2, blk, blk), F32)] + _attn_scratch(blk, 2 * qw),
        compiler_params=_cparams(("arbitrary", "arbitrary")),
        name="diff_attn",
    )(qt, k, vt, gate, bias_table, lam_params, onorm)


def _out_ple_kernel(x_ref, ma_ref, mb_ref, p_ref, wo_ref, wp_ref, gn_ref, wg_ref, o_ref, h_ref):
    half = ma_ref.shape[1]
    d = o_ref.shape[1]
    chunks = [slice(lo, lo + OUT_CHUNK) for lo in range(0, d, OUT_CHUNK)]

    ma, mb = ma_ref[...], mb_ref[...]
    sum_sq = jnp.zeros((ma.shape[0], 1), F32)
    for cols in chunks:
        hc = (x_ref[:, cols] + jnp.dot(ma, wo_ref[:half, cols], preferred_element_type=F32)
              + jnp.dot(mb, wo_ref[half:, cols], preferred_element_type=F32))
        h_ref[:, cols] = hc
        sum_sq = sum_sq + jnp.sum(hc * hc, axis=-1, keepdims=True)

    hn = (h_ref[...] * lax.rsqrt(sum_sq * (1.0 / d) + EPS) * gn_ref[...]).astype(BF16)
    pb = p_ref[...].astype(BF16)
    for cols in chunks:
        zg = jnp.dot(hn, wg_ref[:, cols], preferred_element_type=F32)
        e = jnp.dot(pb, wp_ref[:, cols], preferred_element_type=F32)
        o_ref[:, cols] = h_ref[:, cols] + e * (1.0 / (1.0 + jnp.exp(-zg)))


def _out_ple(x2, mix_a, mix_b, p2, w_out, ple_proj, gate_norm, ple_gate, tm):
    m, d = x2.shape
    half = mix_a.shape[1]
    pdim = p2.shape[1]
    const = lambda i: (0, 0)
    single = dict(pipeline_mode=pl.Buffered(1))
    return pl.pallas_call(
        _out_ple_kernel,
        grid=(m // tm,),
        in_specs=[
            pl.BlockSpec((tm, d), lambda i: (i, 0)),
            pl.BlockSpec((tm, half), lambda i: (i, 0)),
            pl.BlockSpec((tm, half), lambda i: (i, 0)),
            pl.BlockSpec((tm, pdim), lambda i: (i, 0)),
            pl.BlockSpec((2 * half, d), const, **single),
            pl.BlockSpec((pdim, d), const, **single),
            pl.BlockSpec((1, d), const),
            pl.BlockSpec((d, d), const, **single),
        ],
        out_specs=pl.BlockSpec((tm, d), lambda i: (i, 0)),
        out_shape=jax.ShapeDtypeStruct((m, d), F32),
        scratch_shapes=[pltpu.VMEM((tm, d), F32)],
        compiler_params=_cparams(("arbitrary",)),
        name="out_ple",
    )(x2, mix_a, mix_b, p2, w_out, ple_proj, gate_norm, ple_gate)


def _t5_bucket(n):
    max_exact = N_BUCKETS // 2
    nf = jnp.maximum(n, 1).astype(F32)
    large = max_exact + (jnp.log(nf / max_exact) / math.log(MAX_DISTANCE / max_exact)
                         * (N_BUCKETS - max_exact)).astype(jnp.int32)
    large = jnp.minimum(large, N_BUCKETS - 1)
    return jnp.where(n < max_exact, n, large)


def _bias_table(rel_bias, blk):
    assert blk >= MAX_DISTANCE
    table = rel_bias.astype(F32)[_t5_bucket(jnp.arange(2 * blk))].T
    table = (table - rel_bias.astype(F32)[N_BUCKETS - 1][:, None]) * LOG2E
    return table[:, None, :]


def kernel(x, p, attn_norm, w_in, b_forget, fox_q_norm, fox_k_norm, diff_q_norm, diff_k_norm,
           lambda_q1, lambda_k1, lambda_q2, lambda_k2, diff_out_norm, w_out, rel_bias,
           ple_proj, ple_gate_norm, ple_gate):
    b, s, d = x.shape
    depth = w_in.shape[0]
    gw = GROUP_WIDTH
    blk, tm = KEY_BLOCK, PROJ_ROWS
    assert s % (Q_PER_K * blk) == 0 and (Q_PER_K * blk) % tm == 0 and tm % blk == 0 and tm % NORM_ROWS == 0
    bias_table = _bias_table(rel_bias, blk)
    w_t = jnp.swapaxes(w_in.astype(F32), 1, 2)

    h = x.reshape(b * s, d)
    for i in range(depth):
        lambda_init = 0.8 - 0.6 * math.exp(-0.3 * i)
        w_f = jnp.pad(w_t[i, 8 * gw:, :], ((0, LANES - N_HEADS), (0, 0)))
        b_f = jnp.pad(b_forget[i].astype(F32), (0, LANES - N_HEADS)).reshape(1, LANES)
        u, c2 = _norm_forget(h, attn_norm[i].astype(F32).reshape(1, d), w_f, b_f, s, tm)

        row = lambda a: a.astype(F32).reshape(1, -1)
        vec_spec = lambda n: pl.BlockSpec((1, n), lambda r: (0, 0))
        c_spec = pl.BlockSpec((tm, LANES), lambda r: (r, 0))
        fox_scale = HEAD_DIM ** -0.5 * LOG2E
        diff_scale = DIFF_QK ** -0.5 * LOG2E

        tiles = s // tm
        flat = lambda width, dtype: (pl.BlockSpec((tm, width), lambda r: (r, 0)),
                                     jax.ShapeDtypeStruct((b * s, width), dtype))
        qw = Q_PER_K * blk
        per_q = qw // tm
        heads_t = (pl.BlockSpec((1, N_HEADS, 1, HEAD_DIM, tm),
                                lambda r: (r // tiles, 0, (r % tiles) // per_q, 0, (r % tiles) % per_q)),
                   jax.ShapeDtypeStruct((b, N_HEADS, s // qw, HEAD_DIM, qw), BF16))
        v_t = (pl.BlockSpec((1, N_HEADS, tm // blk, V_ROWS, blk), lambda r: (r // tiles, 0, r % tiles, 0, 0)),
               jax.ShapeDtypeStruct((b, N_HEADS, s // blk, V_ROWS, blk), BF16))
        proj = lambda kern, section, extra, specs, out, name: _proj(
            kern, u, w_t, i, section, extra, specs, out[0], out[1], tm, name)
        vt_kernel = functools.partial(_proj_vt_kernel, blk=blk)

        qa = proj(functools.partial(_proj_heads_kernel, scale=fox_scale), 0,
                  [row(fox_q_norm[i])], [vec_spec(HEAD_DIM)], heads_t, "proj_fox_q")
        ka = proj(_proj_fox_key_kernel, 1, [row(fox_k_norm[i]), c2], [vec_spec(HEAD_DIM), c_spec],
                  flat(N_HEADS * AUG, BF16), "proj_fox_k")
        va = proj(vt_kernel, 2, [], [], v_t, "proj_fox_v")
        ga = proj(_proj_silu_kernel, 3, [], [], flat(gw, BF16), "proj_fox_gate")
        dq_gain = row(jnp.tile(diff_q_norm[i], 2))
        dk_gain = row(jnp.tile(diff_k_norm[i], 2))
        qb = proj(functools.partial(_proj_diff_kernel, scale=diff_scale, transposed=True), 4,
                  [dq_gain], [vec_spec(HEAD_DIM)], heads_t, "proj_diff_q")
        kb = proj(functools.partial(_proj_diff_kernel, scale=1.0, transposed=False), 5,
                  [dk_gain], [vec_spec(HEAD_DIM)], flat(gw, BF16), "proj_diff_k")
        vb = proj(vt_kernel, 6, [], [], v_t, "proj_diff_v")
        gb = proj(_proj_silu_kernel, 7, [], [], flat(gw, BF16), "proj_diff_gate")

        to3 = lambda a: a.reshape(b, s, a.shape[-1])
        c_rows = (c2[:, :N_HEADS].reshape(b, s // qw, qw, N_HEADS).transpose(0, 3, 1, 2)
                  .reshape(b, N_HEADS, s // qw, 1, qw))
        mix_a = _fox_attn(qa, c_rows, to3(ka), va, to3(ga), blk)
        lam_params = jnp.stack([lambda_q1[i], lambda_k1[i], lambda_q2[i], lambda_k2[i]]).astype(F32)
        mix_b = _diff_attn(qb, to3(kb), vb, to3(gb), bias_table, lam_params,
                           row(diff_out_norm[i]), blk, lambda_init)

        h = _out_ple(h, mix_a.reshape(b * s, gw), mix_b.reshape(b * s, gw),
                     p[i].reshape(b * s, -1), w_out[i].astype(BF16), ple_proj[i].astype(BF16),
                     row(ple_gate_norm[i]), ple_gate[i].astype(BF16), OUT_ROWS)
    return h.reshape(b, s, d)
```

```python
import functools
import math

import jax
import jax.numpy as jnp
from jax import lax
from jax.experimental import pallas as pl
from jax.experimental.pallas import tpu as pltpu

F32 = jnp.float32
BF16 = jnp.bfloat16

HEAD_DIM = 128
N_HEADS = 8
DIFF_QK = HEAD_DIM // 2
GROUP_WIDTH = N_HEADS * HEAD_DIM
N_BUCKETS = 32
MAX_DISTANCE = 128
EPS = 1e-6
LOG2E = math.log2(math.e)
MASKED = -3e38
M_FLOOR = -(2.0 ** 100)
LANES = 128
AUG = 2 * HEAD_DIM
V_ROWS = HEAD_DIM + 16

KEY_BLOCK = 512
Q_PER_K = 4
QUERY_CHUNK = 256
PROJ_ROWS = 1024
PROJ_CHUNK = 2 * HEAD_DIM
NORM_ROWS = 256
OUT_ROWS = 512
OUT_CHUNK = 512
VMEM_LIMIT = 56 * 1024 * 1024


def _cparams(sem):
    return pltpu.CompilerParams(dimension_semantics=sem, vmem_limit_bytes=VMEM_LIMIT)


_NT = (((1,), (1,)), ((), ()))


def _split3(v):
    p0 = v.astype(BF16).astype(F32)
    r1 = v - p0
    p1 = r1.astype(BF16).astype(F32)
    p2 = (r1 - p1).astype(BF16).astype(F32)
    return p0, p1, p2


def _norm_forget_kernel(x_ref, g_ref, wf_ref, bf_ref, u_ref, c_ref, carry_ref, *, tiles_per_seq):
    i = pl.program_id(0)

    @pl.when(i % tiles_per_seq == 0)
    def _():
        carry_ref[...] = jnp.zeros_like(carry_ref)

    sub = NORM_ROWS
    row = lax.broadcasted_iota(jnp.int32, (sub, sub), 0)
    col = lax.broadcasted_iota(jnp.int32, (sub, sub), 1)
    tri = jnp.where(col <= row, 1.0, 0.0).astype(BF16)
    wf = wf_ref[...].astype(BF16)
    carry = carry_ref[...]
    for r in range(x_ref.shape[0] // sub):
        rows = slice(r * sub, (r + 1) * sub)
        x = x_ref[rows, :]
        ms = jnp.mean(x * x, axis=-1, keepdims=True)
        u = (x * lax.rsqrt(ms + EPS) * g_ref[...]).astype(BF16)
        u_ref[rows, :] = u

        z = lax.dot_general(u, wf, _NT, preferred_element_type=F32) + bf_ref[...]
        lf = (jnp.minimum(z, 0.0) - jnp.log(1.0 + jnp.exp(-jnp.abs(z)))) * LOG2E
        c = carry
        for part in _split3(lf):
            c = c + jnp.dot(tri, part.astype(BF16), preferred_element_type=F32)
        c_ref[rows, :] = c
        carry = c[sub - 1:sub, :]
    carry_ref[...] = carry


def _norm_forget(x2, gain, w_f, b_f, seq, tm):
    m, d = x2.shape
    kern = functools.partial(_norm_forget_kernel, tiles_per_seq=seq // tm)
    return pl.pallas_call(
        kern,
        grid=(m // tm,),
        in_specs=[
            pl.BlockSpec((tm, d), lambda i: (i, 0)),
            pl.BlockSpec((1, d), lambda i: (0, 0)),
            pl.BlockSpec((LANES, d), lambda i: (0, 0)),
            pl.BlockSpec((1, LANES), lambda i: (0, 0)),
        ],
        out_specs=[
            pl.BlockSpec((tm, d), lambda i: (i, 0)),
            pl.BlockSpec((tm, LANES), lambda i: (i, 0)),
        ],
        out_shape=[
            jax.ShapeDtypeStruct((m, d), BF16),
            jax.ShapeDtypeStruct((m, LANES), F32),
        ],
        scratch_shapes=[pltpu.VMEM((1, LANES), F32)],
        compiler_params=_cparams(("arbitrary",)),
        name="norm_forget",
    )(x2, gain, w_f, b_f)


def _proj_chunks(u_ref, w_ref, wb_ref):
    @pl.when(pl.program_id(0) == 0)
    def _():
        wb_ref[...] = w_ref[...].astype(BF16)

    u = u_ref[...]
    for c in range(wb_ref.shape[0] // PROJ_CHUNK):
        cols = slice(c * PROJ_CHUNK, (c + 1) * PROJ_CHUNK)
        yield c * (PROJ_CHUNK // HEAD_DIM), lax.dot_general(u, wb_ref[cols, :], _NT, preferred_element_type=F32)


def _heads_of(h0, y):
    for j in range(PROJ_CHUNK // HEAD_DIM):
        yield h0 + j, y[:, j * HEAD_DIM:(j + 1) * HEAD_DIM]


def _proj_vt_kernel(u_ref, w_ref, o_ref, wb_ref, *, blk):
    tail_row = lax.broadcasted_iota(jnp.int32, (V_ROWS - HEAD_DIM, blk), 0)
    tail = jnp.where(tail_row == 0, 1.0, 0.0).astype(BF16)
    for h0, y in _proj_chunks(u_ref, w_ref, wb_ref):
        for h, yh in _heads_of(h0, y):
            yt = yh.T.astype(BF16)
            for kb in range(yt.shape[1] // blk):
                o_ref[0, h, kb, :HEAD_DIM, :] = yt[:, kb * blk:(kb + 1) * blk]
                o_ref[0, h, kb, HEAD_DIM:, :] = tail


def _proj_silu_kernel(u_ref, w_ref, o_ref, wb_ref):
    for h0, y in _proj_chunks(u_ref, w_ref, wb_ref):
        o_ref[:, h0 * HEAD_DIM:h0 * HEAD_DIM + PROJ_CHUNK] = (y * (1.0 / (1.0 + jnp.exp(-y)))).astype(o_ref.dtype)


def _proj_heads_kernel(u_ref, w_ref, g_ref, o_ref, wb_ref, *, scale):
    gain = g_ref[...] * scale
    for h0, y in _proj_chunks(u_ref, w_ref, wb_ref):
        for h, yh in _heads_of(h0, y):
            ms = jnp.mean(yh * yh, axis=-1, keepdims=True)
            o_ref[0, h, 0] = (yh * lax.rsqrt(ms + EPS) * gain).T.astype(BF16)


def _proj_fox_key_kernel(u_ref, w_ref, g_ref, c_ref, o_ref, wb_ref):
    tm = u_ref.shape[0]
    lane = lax.broadcasted_iota(jnp.int32, (tm, LANES), 1)
    gain = g_ref[...]
    c_all = c_ref[...]
    for h0, y in _proj_chunks(u_ref, w_ref, wb_ref):
        for h, yh in _heads_of(h0, y):
            ms = jnp.mean(yh * yh, axis=-1, keepdims=True)
            o_ref[:, h * AUG:h * AUG + HEAD_DIM] = (yh * lax.rsqrt(ms + EPS) * gain).astype(BF16)
            p0, p1, p2 = _split3(-jnp.broadcast_to(c_all[:, h:h + 1], (tm, LANES)))
            aug = jnp.where(lane == 3, p0, jnp.where(lane == 4, p1, jnp.where(lane == 5, p2, 0.0)))
            aug = jnp.where(lane < 3, 1.0, aug)
            o_ref[:, h * AUG + HEAD_DIM:(h + 1) * AUG] = aug.astype(BF16)


def _proj_diff_kernel(u_ref, w_ref, g_ref, o_ref, wb_ref, *, scale, transposed):
    tm = u_ref.shape[0]
    lane = lax.broadcasted_iota(jnp.int32, (tm, LANES), 1)
    lo = lane < DIFF_QK
    gain = g_ref[...] * scale
    for h0, y in _proj_chunks(u_ref, w_ref, wb_ref):
        for h, yh in _heads_of(h0, y):
            sq = yh * yh
            s_lo = jnp.sum(jnp.where(lo, sq, 0.0), axis=-1, keepdims=True)
            s_all = jnp.sum(sq, axis=-1, keepdims=True)
            ms = jnp.where(lo, s_lo, s_all - s_lo) * (1.0 / DIFF_QK)
            normed = yh * lax.rsqrt(ms + EPS) * gain
            if transposed:
                o_ref[0, h, 0] = normed.T.astype(BF16)
            else:
                o_ref[:, h * HEAD_DIM:(h + 1) * HEAD_DIM] = normed.astype(BF16)


def _proj(kern, u, w_t, layer, section, extra, extra_specs, out_spec, out_shape, tm, name):
    m, d = u.shape
    n = GROUP_WIDTH
    return pl.pallas_call(
        kern,
        grid=(m // tm,),
        in_specs=[
            pl.BlockSpec((tm, d), lambda i: (i, 0)),
            pl.BlockSpec((None, n, d), lambda i: (layer, section, 0), pipeline_mode=pl.Buffered(1)),
        ] + extra_specs,
        out_specs=out_spec,
        out_shape=out_shape,
        scratch_shapes=[pltpu.VMEM((n, d), BF16)],
        compiler_params=_cparams(("arbitrary",)),
        name=name,
    )(u, w_t, *extra)


def _attn_scratch(blk, width):
    return [
        pltpu.VMEM((blk, width), F32),
        pltpu.VMEM((1, width), F32),
        pltpu.VMEM((2, blk, width), BF16),
        pltpu.VMEM((2, 1, width), F32),
        pltpu.VMEM((1, width), F32),
        pltpu.VMEM((V_ROWS, width), F32),
    ]


def _query_chunks(blk, n_maps):
    qw = Q_PER_K * blk
    return [slice(lo, lo + QUERY_CHUNK) for lo in range(0, n_maps * qw, QUERY_CHUNK)]


def _attn_pipeline(qi, scores, vt_of, blk, chunks, s_ref, mx_ref, p_ref, al_ref, m_ref, acc_ref):
    n = Q_PER_K * (qi + 1)
    qw = Q_PER_K * blk

    def key_block(t):
        return jnp.maximum(n - 1 - t, 0)

    def ahead(t, lanes):
        return (lanes.start % qw) // blk - (Q_PER_K - 1 - t)

    def qk(t, lanes, far=False):
        st = scores(key_block(t), 2 if far else min(ahead(t, lanes), 2), lanes)
        s_ref[:, lanes] = st
        mx_ref[:, lanes] = jnp.max(st, axis=0, keepdims=True)

    def softmax(slot, lanes):
        m_old = m_ref[:, lanes]
        m_new = jnp.maximum(m_old, mx_ref[:, lanes])
        m_ref[:, lanes] = m_new
        al_ref[slot, :, lanes] = jnp.exp2(m_old - m_new)
        p_ref[slot, :, lanes] = jnp.exp2(s_ref[:, lanes] - m_new).astype(BF16)

    def pv(t, slot, lanes):
        acc_ref[:, lanes] = al_ref[slot, :, lanes] * acc_ref[:, lanes] + jnp.dot(
            vt_of(key_block(t)), p_ref[slot, :, lanes], preferred_element_type=F32)

    def step(t, slot, with_qk=True, static_t=None):
        for lanes in chunks:
            if static_t is None or ahead(static_t, lanes) >= 0:
                softmax(slot, lanes)
            if static_t is None or ahead(static_t - 1, lanes) >= 0:
                pv(t - 1, 1 - slot, lanes)
            if with_qk and (static_t is None or ahead(static_t + 1, lanes) >= 0):
                qk(t + 1, lanes, far=static_t is None)

    m_ref[...] = jnp.full(m_ref.shape, M_FLOOR, F32)
    acc_ref[...] = jnp.zeros(acc_ref.shape, F32)

    for lanes in chunks:
        if ahead(0, lanes) >= 0:
            qk(0, lanes)
    for t in range(Q_PER_K):
        step(t, t % 2, static_t=t)

    assert Q_PER_K % 2 == 0
    pairs = jnp.maximum(Q_PER_K // 2 * qi - 1, 0)

    @pl.loop(0, pairs // 2)
    def _(u):
        for i in range(4):
            step(4 * u + Q_PER_K + i, i % 2)

    @pl.when(pairs % 2 == 1)
    def _():
        step(n - 4, 0)
        step(n - 3, 1)

    @pl.when(qi >= 1)
    def _():
        step(n - 2, 0)
        step(n - 1, 1, with_qk=False)

    for lanes in chunks:
        pv(n - 1, 1, lanes)


def _fox_attn_kernel(q_ref, c_ref, k_ref, vt_ref, gate_ref, o_ref, *scratch, blk):
    qw = Q_PER_K * blk
    key = lax.broadcasted_iota(jnp.int32, (blk, blk), 0)
    qry = lax.broadcasted_iota(jnp.int32, (blk, blk), 1)
    causal = key <= qry
    dim = lax.broadcasted_iota(jnp.int32, (AUG - HEAD_DIM, qw), 0)
    chunks = _query_chunks(blk, 1)

    @pl.loop(0, q_ref.shape[2])
    def _(qi):
        p0, p1, p2 = _split3(c_ref[0, 0, qi])
        decay = jnp.where(dim == 0, p0, jnp.where(dim == 1, p1, jnp.where(dim == 2, p2, 0.0)))
        decay = jnp.where((dim >= 3) & (dim < 6), 1.0, decay)
        qt = jnp.concatenate([q_ref[0, 0, qi], decay.astype(BF16)], axis=0)

        def scores(kb, ahead, lanes):
            k = k_ref[0, pl.ds(pl.multiple_of(kb * blk, blk), blk), :]
            st = jnp.dot(k, qt[:, lanes], preferred_element_type=F32)
            if ahead == 0:
                within = slice(lanes.start % blk, lanes.start % blk + QUERY_CHUNK)
                return jnp.where(causal[:, within], st, MASKED)
            return st

        _attn_pipeline(qi, scores, lambda kb: vt_ref[0, 0, kb], blk, chunks, *scratch)
        acc = scratch[-1][...]
        ot = acc[:HEAD_DIM] / acc[HEAD_DIM:HEAD_DIM + 1]
        rows = pl.ds(pl.multiple_of(qi * qw, qw), qw)
        o_ref[0, rows, :] = (ot.T * gate_ref[0, rows, :]).astype(o_ref.dtype)


def _fox_attn(qt, c_rows, k, vt, gate, blk):
    b, s, _ = k.shape
    qw = Q_PER_K * blk
    head = lambda *block: pl.BlockSpec((1, 1) + block, lambda bi, h: (bi, h) + (0,) * len(block))
    cols = lambda width: pl.BlockSpec((1, s, width), lambda bi, h: (bi, 0, h))
    return pl.pallas_call(
        functools.partial(_fox_attn_kernel, blk=blk),
        grid=(b, N_HEADS),
        in_specs=[
            head(s // qw, HEAD_DIM, qw),
            head(s // qw, 1, qw),
            cols(AUG),
            head(s // blk, V_ROWS, blk),
            cols(HEAD_DIM),
        ],
        out_specs=cols(HEAD_DIM),
        out_shape=jax.ShapeDtypeStruct((b, s, GROUP_WIDTH), BF16),
        scratch_shapes=_attn_scratch(blk, qw),
        compiler_params=_cparams(("arbitrary", "arbitrary")),
        name="fox_attn",
    )(qt, c_rows, k, vt, gate)


def _diff_attn_kernel(q_ref, k_ref, vt_ref, gate_ref, table_ref, lam_ref, onorm_ref, o_ref, bias_ref,
                      *scratch, blk, lambda_init):
    qw = Q_PER_K * blk
    shifted = pltpu.roll(jnp.broadcast_to(table_ref[0], (blk, 2 * blk)), 0, 1, stride=1, stride_axis=0)
    key = lax.broadcasted_iota(jnp.int32, (blk, blk), 0)
    qry = lax.broadcasted_iota(jnp.int32, (blk, blk), 1)
    bias_ref[0] = jnp.where(key <= qry, shifted[:, :blk], MASKED)
    bias_ref[1] = shifted[:, blk:]

    lam_p = lam_ref[...]
    lam = (jnp.exp(jnp.sum(lam_p[0:1] * lam_p[1:2], axis=-1, keepdims=True))
           - jnp.exp(jnp.sum(lam_p[2:3] * lam_p[3:4], axis=-1, keepdims=True)) + lambda_init)
    zeros = jnp.zeros((DIFF_QK, qw), BF16)
    chunks = _query_chunks(blk, 2)

    @pl.loop(0, q_ref.shape[2])
    def _(qi):
        qt = q_ref[0, 0, qi]
        qzt = jnp.concatenate([jnp.concatenate([qt[:DIFF_QK], zeros], axis=0),
                               jnp.concatenate([zeros, qt[DIFF_QK:]], axis=0)], axis=1)

        def scores(kb, ahead, lanes):
            k = k_ref[0, pl.ds(pl.multiple_of(kb * blk, blk), blk), :]
            st = jnp.dot(k, qzt[:, lanes], preferred_element_type=F32)
            if ahead < 2:
                within = slice(lanes.start % blk, lanes.start % blk + QUERY_CHUNK)
                return st + bias_ref[ahead, :, within]
            return st

        _attn_pipeline(qi, scores, lambda kb: vt_ref[0, 0, kb], blk, chunks, *scratch)

        acc = scratch[-1][...]
        ot = acc[:HEAD_DIM] / acc[HEAD_DIM:HEAD_DIM + 1]
        o = (ot[:, :qw] - lam * ot[:, qw:]).T
        ms = jnp.mean(o * o, axis=-1, keepdims=True)
        o = o * lax.rsqrt(ms + EPS) * onorm_ref[...] * (1.0 - lambda_init)
        rows = pl.ds(pl.multiple_of(qi * qw, qw), qw)
        o_ref[0, rows, :] = (o * gate_ref[0, rows, :]).astype(o_ref.dtype)


def _diff_attn(qt, k, vt, gate, bias_table, lam_params, onorm, blk, lambda_init):
    b, s, _ = k.shape
    qw = Q_PER_K * blk
    head = lambda *block: pl.BlockSpec((1, 1) + block, lambda bi, h: (bi, h) + (0,) * len(block))
    cols = lambda width: pl.BlockSpec((1, s, width), lambda bi, h: (bi, 0, h))
    return pl.pallas_call(
        functools.partial(_diff_attn_kernel, blk=blk, lambda_init=lambda_init),
        grid=(b, N_HEADS),
        in_specs=[
            head(s // qw, HEAD_DIM, qw),
            cols(HEAD_DIM),
            head(s // blk, V_ROWS, blk),
            cols(HEAD_DIM),
            pl.BlockSpec((1, 1, 2 * blk), lambda bi, h: (h, 0, 0)),
            pl.BlockSpec((4, DIFF_QK), lambda bi, h: (0, 0)),
            pl.BlockSpec((1, HEAD_DIM), lambda bi, h: (0, 0)),
        ],
        out_specs=cols(HEAD_DIM),
        out_shape=jax.ShapeDtypeStruct((b, s, GROUP_WIDTH), BF16),
        scratch_shapes=[pltpu.VMEM((2, blk, blk), F32)] + _attn_scratch(blk, 2 * qw),
        compiler_params=_cparams(("arbitrary", "arbitrary")),
        name="diff_attn",
    )(qt, k, vt, gate, bias_table, lam_params, onorm)


def _out_ple_kernel(x_ref, ma_ref, mb_ref, p_ref, wo_ref, wp_ref, gn_ref, wg_ref, o_ref, h_ref):
    half = ma_ref.shape[1]
    d = o_ref.shape[1]
    chunks = [slice(lo, lo + OUT_CHUNK) for lo in range(0, d, OUT_CHUNK)]

    ma, mb = ma_ref[...], mb_ref[...]
    sum_sq = jnp.zeros((ma.shape[0], 1), F32)
    for cols in chunks:
        hc = (x_ref[:, cols] + jnp.dot(ma, wo_ref[:half, cols], preferred_element_type=F32)
              + jnp.dot(mb, wo_ref[half:, cols], preferred_element_type=F32))
        h_ref[:, cols] = hc
        sum_sq = sum_sq + jnp.sum(hc * hc, axis=-1, keepdims=True)

    hn = (h_ref[...] * lax.rsqrt(sum_sq * (1.0 / d) + EPS) * gn_ref[...]).astype(BF16)
    pb = p_ref[...].astype(BF16)
    for cols in chunks:
        zg = jnp.dot(hn, wg_ref[:, cols], preferred_element_type=F32)
        e = jnp.dot(pb, wp_ref[:, cols], preferred_element_type=F32)
        o_ref[:, cols] = h_ref[:, cols] + e * (1.0 / (1.0 + jnp.exp(-zg)))


def _out_ple(x2, mix_a, mix_b, p2, w_out, ple_proj, gate_norm, ple_gate, tm):
    m, d = x2.shape
    half = mix_a.shape[1]
    pdim = p2.shape[1]
    const = lambda i: (0, 0)
    single = dict(pipeline_mode=pl.Buffered(1))
    return pl.pallas_call(
        _out_ple_kernel,
        grid=(m // tm,),
        in_specs=[
            pl.BlockSpec((tm, d), lambda i: (i, 0)),
            pl.BlockSpec((tm, half), lambda i: (i, 0)),
            pl.BlockSpec((tm, half), lambda i: (i, 0)),
            pl.BlockSpec((tm, pdim), lambda i: (i, 0)),
            pl.BlockSpec((2 * half, d), const, **single),
            pl.BlockSpec((pdim, d), const, **single),
            pl.BlockSpec((1, d), const),
            pl.BlockSpec((d, d), const, **single),
        ],
        out_specs=pl.BlockSpec((tm, d), lambda i: (i, 0)),
        out_shape=jax.ShapeDtypeStruct((m, d), F32),
        scratch_shapes=[pltpu.VMEM((tm, d), F32)],
        compiler_params=_cparams(("arbitrary",)),
        name="out_ple",
    )(x2, mix_a, mix_b, p2, w_out, ple_proj, gate_norm, ple_gate)


def _t5_bucket(n):
    max_exact = N_BUCKETS // 2
    nf = jnp.maximum(n, 1).astype(F32)
    large = max_exact + (jnp.log(nf / max_exact) / math.log(MAX_DISTANCE / max_exact)
                         * (N_BUCKETS - max_exact)).astype(jnp.int32)
    large = jnp.minimum(large, N_BUCKETS - 1)
    return jnp.where(n < max_exact, n, large)


def _bias_table(rel_bias, blk):
    assert blk >= MAX_DISTANCE
    table = rel_bias.astype(F32)[_t5_bucket(jnp.arange(2 * blk))].T
    table = (table - rel_bias.astype(F32)[N_BUCKETS - 1][:, None]) * LOG2E
    return table[:, None, :]


def kernel(x, p, attn_norm, w_in, b_forget, fox_q_norm, fox_k_norm, diff_q_norm, diff_k_norm,
           lambda_q1, lambda_k1, lambda_q2, lambda_k2, diff_out_norm, w_out, rel_bias,
           ple_proj, ple_gate_norm, ple_gate):
    b, s, d = x.shape
    depth = w_in.shape[0]
    gw = GROUP_WIDTH
    blk, tm = KEY_BLOCK, PROJ_ROWS
    assert s % (Q_PER_K * blk) == 0 and (Q_PER_K * blk) % tm == 0 and tm % blk == 0 and tm % NORM_ROWS == 0
    bias_table = _bias_table(rel_bias, blk)
    w_t = jnp.swapaxes(w_in.astype(F32), 1, 2)

    h = x.reshape(b * s, d)
    for i in range(depth):
        lambda_init = 0.8 - 0.6 * math.exp(-0.3 * i)
        w_f = jnp.pad(w_t[i, 8 * gw:, :], ((0, LANES - N_HEADS), (0, 0)))
        b_f = jnp.pad(b_forget[i].astype(F32), (0, LANES - N_HEADS)).reshape(1, LANES)
        u, c2 = _norm_forget(h, attn_norm[i].astype(F32).reshape(1, d), w_f, b_f, s, tm)

        row = lambda a: a.astype(F32).reshape(1, -1)
        vec_spec = lambda n: pl.BlockSpec((1, n), lambda r: (0, 0))
        c_spec = pl.BlockSpec((tm, LANES), lambda r: (r, 0))
        fox_scale = HEAD_DIM ** -0.5 * LOG2E
        diff_scale = DIFF_QK ** -0.5 * LOG2E

        tiles = s // tm
        flat = lambda width, dtype: (pl.BlockSpec((tm, width), lambda r: (r, 0)),
                                     jax.ShapeDtypeStruct((b * s, width), dtype))
        qw = Q_PER_K * blk
        per_q = qw // tm
        heads_t = (pl.BlockSpec((1, N_HEADS, 1, HEAD_DIM, tm),
                                lambda r: (r // tiles, 0, (r % tiles) // per_q, 0, (r % tiles) % per_q)),
                   jax.ShapeDtypeStruct((b, N_HEADS, s // qw, HEAD_DIM, qw), BF16))
        v_t = (pl.BlockSpec((1, N_HEADS, tm // blk, V_ROWS, blk), lambda r: (r // tiles, 0, r % tiles, 0, 0)),
               jax.ShapeDtypeStruct((b, N_HEADS, s // blk, V_ROWS, blk), BF16))
        proj = lambda kern, section, extra, specs, out, name: _proj(
            kern, u, w_t, i, section, extra, specs, out[0], out[1], tm, name)
        vt_kernel = functools.partial(_proj_vt_kernel, blk=blk)

        qa = proj(functools.partial(_proj_heads_kernel, scale=fox_scale), 0,
                  [row(fox_q_norm[i])], [vec_spec(HEAD_DIM)], heads_t, "proj_fox_q")
        ka = proj(_proj_fox_key_kernel, 1, [row(fox_k_norm[i]), c2], [vec_spec(HEAD_DIM), c_spec],
                  flat(N_HEADS * AUG, BF16), "proj_fox_k")
        va = proj(vt_kernel, 2, [], [], v_t, "proj_fox_v")
        ga = proj(_proj_silu_kernel, 3, [], [], flat(gw, BF16), "proj_fox_gate")
        dq_gain = row(jnp.tile(diff_q_norm[i], 2))
        dk_gain = row(jnp.tile(diff_k_norm[i], 2))
        qb = proj(functools.partial(_proj_diff_kernel, scale=diff_scale, transposed=True), 4,
                  [dq_gain], [vec_spec(HEAD_DIM)], heads_t, "proj_diff_q")
        kb = proj(functools.partial(_proj_diff_kernel, scale=1.0, transposed=False), 5,
                  [dk_gain], [vec_spec(HEAD_DIM)], flat(gw, BF16), "proj_diff_k")
        vb = proj(vt_kernel, 6, [], [], v_t, "proj_diff_v")
        gb = proj(_proj_silu_kernel, 7, [], [], flat(gw, BF16), "proj_diff_gate")

        to3 = lambda a: a.reshape(b, s, a.shape[-1])
        c_rows = (c2[:, :N_HEADS].reshape(b, s // qw, qw, N_HEADS).transpose(0, 3, 1, 2)
                  .reshape(b, N_HEADS, s // qw, 1, qw))
        mix_a = _fox_attn(qa, c_rows, to3(ka), va, to3(ga), blk)
        lam_params = jnp.stack([lambda_q1[i], lambda_k1[i], lambda_q2[i], lambda_k2[i]]).astype(F32)
        mix_b = _diff_attn(qb, to3(kb), vb, to3(gb), bias_table, lam_params,
                           row(diff_out_norm[i]), blk, lambda_init)

        h = _out_ple(h, mix_a.reshape(b * s, gw), mix_b.reshape(b * s, gw),
                     p[i].reshape(b * s, -1), w_out[i].astype(BF16), ple_proj[i].astype(BF16),
                     row(ple_gate_norm[i]), ple_gate[i].astype(BF16), OUT_ROWS)
    return h.reshape(b, s, d)
```

```python
import functools
import math

import jax
import jax.numpy as jnp
from jax import lax
from jax.experimental import pallas as pl
from jax.experimental.pallas import tpu as pltpu

F32 = jnp.float32
BF16 = jnp.bfloat16

HEAD_DIM = 128
N_HEADS = 8
DIFF_QK = HEAD_DIM // 2
GROUP_WIDTH = N_HEADS * HEAD_DIM
N_BUCKETS = 32
MAX_DISTANCE = 128
EPS = 1e-6
LOG2E = math.log2(math.e)
MASKED = -3e38
M_FLOOR = -(2.0 ** 100)
LANES = 128
AUG = 2 * HEAD_DIM
V_ROWS = HEAD_DIM + 16

KEY_BLOCK = 512
Q_PER_K = 4
QUERY_CHUNK = 256
PROJ_ROWS = 1024
PROJ_CHUNK = 2 * HEAD_DIM
NORM_ROWS = 256
OUT_ROWS = 512
OUT_CHUNK = 512
VMEM_LIMIT = 56 * 1024 * 1024


def _cparams(sem):
    return pltpu.CompilerParams(dimension_semantics=sem, vmem_limit_bytes=VMEM_LIMIT)


_NT = (((1,), (1,)), ((), ()))


def _split3(v):
    p0 = v.astype(BF16).astype(F32)
    r1 = v - p0
    p1 = r1.astype(BF16).astype(F32)
    p2 = (r1 - p1).astype(BF16).astype(F32)
    return p0, p1, p2


def _norm_forget_kernel(x_ref, g_ref, wf_ref, bf_ref, u_ref, c_ref, carry_ref, *, tiles_per_seq):
    i = pl.program_id(0)

    @pl.when(i % tiles_per_seq == 0)
    def _():
        carry_ref[...] = jnp.zeros_like(carry_ref)

    sub = NORM_ROWS
    row = lax.broadcasted_iota(jnp.int32, (sub, sub), 0)
    col = lax.broadcasted_iota(jnp.int32, (sub, sub), 1)
    tri = jnp.where(col <= row, 1.0, 0.0).astype(BF16)
    wf = wf_ref[...].astype(BF16)
    carry = carry_ref[...]
    for r in range(x_ref.shape[0] // sub):
        rows = slice(r * sub, (r + 1) * sub)
        x = x_ref[rows, :]
        ms = jnp.mean(x * x, axis=-1, keepdims=True)
        u = (x * lax.rsqrt(ms + EPS) * g_ref[...]).astype(BF16)
        u_ref[rows, :] = u

        z = lax.dot_general(u, wf, _NT, preferred_element_type=F32) + bf_ref[...]
        lf = (jnp.minimum(z, 0.0) - jnp.log(1.0 + jnp.exp(-jnp.abs(z)))) * LOG2E
        c = carry
        for part in _split3(lf):
            c = c + jnp.dot(tri, part.astype(BF16), preferred_element_type=F32)
        c_ref[rows, :] = c
        carry = c[sub - 1:sub, :]
    carry_ref[...] = carry


def _norm_forget(x2, gain, w_f, b_f, seq, tm):
    m, d = x2.shape
    kern = functools.partial(_norm_forget_kernel, tiles_per_seq=seq // tm)
    return pl.pallas_call(
        kern,
        grid=(m // tm,),
        in_specs=[
            pl.BlockSpec((tm, d), lambda i: (i, 0)),
            pl.BlockSpec((1, d), lambda i: (0, 0)),
            pl.BlockSpec((LANES, d), lambda i: (0, 0)),
            pl.BlockSpec((1, LANES), lambda i: (0, 0)),
        ],
        out_specs=[
            pl.BlockSpec((tm, d), lambda i: (i, 0)),
            pl.BlockSpec((tm, LANES), lambda i: (i, 0)),
        ],
        out_shape=[
            jax.ShapeDtypeStruct((m, d), BF16),
            jax.ShapeDtypeStruct((m, LANES), F32),
        ],
        scratch_shapes=[pltpu.VMEM((1, LANES), F32)],
        compiler_params=_cparams(("arbitrary",)),
        name="norm_forget",
    )(x2, gain, w_f, b_f)


def _proj_chunks(u_ref, w_ref, wb_ref):
    @pl.when(pl.program_id(0) == 0)
    def _():
        wb_ref[...] = w_ref[...].astype(BF16)

    u = u_ref[...]
    for c in range(wb_ref.shape[0] // PROJ_CHUNK):
        cols = slice(c * PROJ_CHUNK, (c + 1) * PROJ_CHUNK)
        yield c * (PROJ_CHUNK // HEAD_DIM), lax.dot_general(u, wb_ref[cols, :], _NT, preferred_element_type=F32)


def _heads_of(h0, y):
    for j in range(PROJ_CHUNK // HEAD_DIM):
        yield h0 + j, y[:, j * HEAD_DIM:(j + 1) * HEAD_DIM]


def _proj_vt_kernel(u_ref, w_ref, o_ref, wb_ref, *, blk):
    tail_row = lax.broadcasted_iota(jnp.int32, (V_ROWS - HEAD_DIM, blk), 0)
    tail = jnp.where(tail_row == 0, 1.0, 0.0).astype(BF16)
    for h0, y in _proj_chunks(u_ref, w_ref, wb_ref):
        for h, yh in _heads_of(h0, y):
            yt = yh.T.astype(BF16)
            for kb in range(yt.shape[1] // blk):
                o_ref[0, h, kb, :HEAD_DIM, :] = yt[:, kb * blk:(kb + 1) * blk]
                o_ref[0, h, kb, HEAD_DIM:, :] = tail


def _proj_silu_kernel(u_ref, w_ref, o_ref, wb_ref):
    for h0, y in _proj_chunks(u_ref, w_ref, wb_ref):
        o_ref[:, h0 * HEAD_DIM:h0 * HEAD_DIM + PROJ_CHUNK] = (y * (1.0 / (1.0 + jnp.exp(-y)))).astype(o_ref.dtype)


def _proj_heads_kernel(u_ref, w_ref, g_ref, o_ref, wb_ref, *, scale):
    gain = g_ref[...] * scale
    for h0, y in _proj_chunks(u_ref, w_ref, wb_ref):
        for h, yh in _heads_of(h0, y):
            ms = jnp.mean(yh * yh, axis=-1, keepdims=True)
            o_ref[0, h, 0] = (yh * lax.rsqrt(ms + EPS) * gain).T.astype(BF16)


def _proj_fox_key_kernel(u_ref, w_ref, g_ref, c_ref, o_ref, wb_ref):
    tm = u_ref.shape[0]
    lane = lax.broadcasted_iota(jnp.int32, (tm, LANES), 1)
    gain = g_ref[...]
    c_all = c_ref[...]
    for h0, y in _proj_chunks(u_ref, w_ref, wb_ref):
        for h, yh in _heads_of(h0, y):
            ms = jnp.mean(yh * yh, axis=-1, keepdims=True)
            o_ref[:, h * AUG:h * AUG + HEAD_DIM] = (yh * lax.rsqrt(ms + EPS) * gain).astype(BF16)
            p0, p1, p2 = _split3(-jnp.broadcast_to(c_all[:, h:h + 1], (tm, LANES)))
            aug = jnp.where(lane == 3, p0, jnp.where(lane == 4, p1, jnp.where(lane == 5, p2, 0.0)))
            aug = jnp.where(lane < 3, 1.0, aug)
            o_ref[:, h * AUG + HEAD_DIM:(h + 1) * AUG] = aug.astype(BF16)


def _proj_diff_kernel(u_ref, w_ref, g_ref, o_ref, wb_ref, *, scale, transposed):
    tm = u_ref.shape[0]
    lane = lax.broadcasted_iota(jnp.int32, (tm, LANES), 1)
    lo = lane < DIFF_QK
    gain = g_ref[...] * scale
    for h0, y in _proj_chunks(u_ref, w_ref, wb_ref):
        for h, yh in _heads_of(h0, y):
            sq = yh * yh
            s_lo = jnp.sum(jnp.where(lo, sq, 0.0), axis=-1, keepdims=True)
            s_all = jnp.sum(sq, axis=-1, keepdims=True)
            ms = jnp.where(lo, s_lo, s_all - s_lo) * (1.0 / DIFF_QK)
            normed = yh * lax.rsqrt(ms + EPS) * gain
            if transposed:
                o_ref[0, h, 0] = normed.T.astype(BF16)
            else:
                o_ref[:, h * HEAD_DIM:(h + 1) * HEAD_DIM] = normed.astype(BF16)


def _proj(kern, u, w_t, layer, section, extra, extra_specs, out_spec, out_shape, tm, name):
    m, d = u.shape
    n = GROUP_WIDTH
    return pl.pallas_call(
        kern,
        grid=(m // tm,),
        in_specs=[
            pl.BlockSpec((tm, d), lambda i: (i, 0)),
            pl.BlockSpec((None, n, d), lambda i: (layer, section, 0), pipeline_mode=pl.Buffered(1)),
        ] + extra_specs,
        out_specs=out_spec,
        out_shape=out_shape,
        scratch_shapes=[pltpu.VMEM((n, d), BF16)],
        compiler_params=_cparams(("arbitrary",)),
        name=name,
    )(u, w_t, *extra)


def _attn_scratch(blk, width):
    return [
        pltpu.VMEM((blk, width), F32),
        pltpu.VMEM((1, width), F32),
        pltpu.VMEM((2, blk, width), BF16),
        pltpu.VMEM((2, 1, width), F32),
        pltpu.VMEM((1, width), F32),
        pltpu.VMEM((V_ROWS, width), F32),
    ]


def _query_chunks(blk, n_maps):
    qw = Q_PER_K * blk
    return [slice(lo, lo + QUERY_CHUNK) for lo in range(0, n_maps * qw, QUERY_CHUNK)]


def _attn_pipeline(qi, scores, vt_of, blk, chunks, s_ref, mx_ref, p_ref, al_ref, m_ref, acc_ref):
    n = Q_PER_K * (qi + 1)
    qw = Q_PER_K * blk

    def key_block(t):
        return jnp.maximum(n - 1 - t, 0)

    def ahead(t, lanes):
        return (lanes.start % qw) // blk - (Q_PER_K - 1 - t)

    def qk(t, lanes, far=False):
        st = scores(key_block(t), 2 if far else min(ahead(t, lanes), 2), lanes)
        s_ref[:, lanes] = st
        mx_ref[:, lanes] = jnp.max(st, axis=0, keepdims=True)

    def softmax(slot, lanes):
        m_old = m_ref[:, lanes]
        m_new = jnp.maximum(m_old, mx_ref[:, lanes])
        m_ref[:, lanes] = m_new
        al_ref[slot, :, lanes] = jnp.exp2(m_old - m_new)
        p_ref[slot, :, lanes] = jnp.exp2(s_ref[:, lanes] - m_new).astype(BF16)

    def pv(t, slot, lanes):
        acc_ref[:, lanes] = al_ref[slot, :, lanes] * acc_ref[:, lanes] + jnp.dot(
            vt_of(key_block(t)), p_ref[slot, :, lanes], preferred_element_type=F32)

    def step(t, slot, with_qk=True, static_t=None):
        for lanes in chunks:
            if static_t is None or ahead(static_t, lanes) >= 0:
                softmax(slot, lanes)
            if static_t is None or ahead(static_t - 1, lanes) >= 0:
                pv(t - 1, 1 - slot, lanes)
            if with_qk and (static_t is None or ahead(static_t + 1, lanes) >= 0):
                qk(t + 1, lanes, far=static_t is None)

    m_ref[...] = jnp.full(m_ref.shape, M_FLOOR, F32)
    acc_ref[...] = jnp.zeros(acc_ref.shape, F32)

    for lanes in chunks:
        if ahead(0, lanes) >= 0:
            qk(0, lanes)
    for t in range(Q_PER_K):
        step(t, t % 2, static_t=t)

    assert Q_PER_K % 4 == 0
    @pl.loop(0, Q_PER_K // 4 * qi - 1)
    def _(u):
        for i in range(4):
            step(4 * u + Q_PER_K + i, i % 2)

    @pl.when(qi >= 1)
    def _():
        for i in range(4):
            step(n - 4 + i, i % 2, with_qk=i < 3)

    for lanes in chunks:
        pv(n - 1, 1, lanes)


def _fox_attn_kernel(q_ref, c_ref, k_ref, vt_ref, gate_ref, o_ref, *scratch, blk):
    qw = Q_PER_K * blk
    key = lax.broadcasted_iota(jnp.int32, (blk, blk), 0)
    qry = lax.broadcasted_iota(jnp.int32, (blk, blk), 1)
    causal = key <= qry
    dim = lax.broadcasted_iota(jnp.int32, (AUG - HEAD_DIM, qw), 0)
    chunks = _query_chunks(blk, 1)

    @pl.loop(0, q_ref.shape[2])
    def _(qi):
        p0, p1, p2 = _split3(c_ref[0, 0, qi])
        decay = jnp.where(dim == 0, p0, jnp.where(dim == 1, p1, jnp.where(dim == 2, p2, 0.0)))
        decay = jnp.where((dim >= 3) & (dim < 6), 1.0, decay)
        qt = jnp.concatenate([q_ref[0, 0, qi], decay.astype(BF16)], axis=0)

        def scores(kb, ahead, lanes):
            k = k_ref[0, pl.ds(pl.multiple_of(kb * blk, blk), blk), :]
            st = jnp.dot(k, qt[:, lanes], preferred_element_type=F32)
            if ahead == 0:
                within = slice(lanes.start % blk, lanes.start % blk + QUERY_CHUNK)
                return jnp.where(causal[:, within], st, MASKED)
            return st

        _attn_pipeline(qi, scores, lambda kb: vt_ref[0, 0, kb], blk, chunks, *scratch)
        acc = scratch[-1][...]
        ot = acc[:HEAD_DIM] / acc[HEAD_DIM:HEAD_DIM + 1]
        rows = pl.ds(pl.multiple_of(qi * qw, qw), qw)
        o_ref[0, rows, :] = (ot.T * gate_ref[0, rows, :]).astype(o_ref.dtype)


def _fox_attn(qt, c_rows, k, vt, gate, blk):
    b, s, _ = k.shape
    qw = Q_PER_K * blk
    head = lambda *block: pl.BlockSpec((1, 1) + block, lambda bi, h: (bi, h) + (0,) * len(block))
    cols = lambda width: pl.BlockSpec((1, s, width), lambda bi, h: (bi, 0, h))
    return pl.pallas_call(
        functools.partial(_fox_attn_kernel, blk=blk),
        grid=(b, N_HEADS),
        in_specs=[
            head(s // qw, HEAD_DIM, qw),
            head(s // qw, 1, qw),
            cols(AUG),
            head(s // blk, V_ROWS, blk),
            cols(HEAD_DIM),
        ],
        out_specs=cols(HEAD_DIM),
        out_shape=jax.ShapeDtypeStruct((b, s, GROUP_WIDTH), BF16),
        scratch_shapes=_attn_scratch(blk, qw),
        compiler_params=_cparams(("arbitrary", "arbitrary")),
        name="fox_attn",
    )(qt, c_rows, k, vt, gate)


def _diff_attn_kernel(q_ref, k_ref, vt_ref, gate_ref, table_ref, lam_ref, onorm_ref, o_ref, bias_ref,
                      *scratch, blk, lambda_init):
    qw = Q_PER_K * blk
    shifted = pltpu.roll(jnp.broadcast_to(table_ref[0], (blk, 2 * blk)), 0, 1, stride=1, stride_axis=0)
    key = lax.broadcasted_iota(jnp.int32, (blk, blk), 0)
    qry = lax.broadcasted_iota(jnp.int32, (blk, blk), 1)
    bias_ref[0] = jnp.where(key <= qry, shifted[:, :blk], MASKED)
    bias_ref[1] = shifted[:, blk:]

    lam_p = lam_ref[...]
    lam = (jnp.exp(jnp.sum(lam_p[0:1] * lam_p[1:2], axis=-1, keepdims=True))
           - jnp.exp(jnp.sum(lam_p[2:3] * lam_p[3:4], axis=-1, keepdims=True)) + lambda_init)
    zeros = jnp.zeros((DIFF_QK, qw), BF16)
    chunks = _query_chunks(blk, 2)

    @pl.loop(0, q_ref.shape[2])
    def _(qi):
        qt = q_ref[0, 0, qi]
        qzt = jnp.concatenate([jnp.concatenate([qt[:DIFF_QK], zeros], axis=0),
                               jnp.concatenate([zeros, qt[DIFF_QK:]], axis=0)], axis=1)

        def scores(kb, ahead, lanes):
            k = k_ref[0, pl.ds(pl.multiple_of(kb * blk, blk), blk), :]
            st = jnp.dot(k, qzt[:, lanes], preferred_element_type=F32)
            if ahead < 2:
                within = slice(lanes.start % blk, lanes.start % blk + QUERY_CHUNK)
                return st + bias_ref[ahead, :, within]
            return st

        _attn_pipeline(qi, scores, lambda kb: vt_ref[0, 0, kb], blk, chunks, *scratch)

        acc = scratch[-1][...]
        ot = acc[:HEAD_DIM] / acc[HEAD_DIM:HEAD_DIM + 1]
        o = (ot[:, :qw] - lam * ot[:, qw:]).T
        ms = jnp.mean(o * o, axis=-1, keepdims=True)
        o = o * lax.rsqrt(ms + EPS) * onorm_ref[...] * (1.0 - lambda_init)
        rows = pl.ds(pl.multiple_of(qi * qw, qw), qw)
        o_ref[0, rows, :] = (o * gate_ref[0, rows, :]).astype(o_ref.dtype)


def _diff_attn(qt, k, vt, gate, bias_table, lam_params, onorm, blk, lambda_init):
    b, s, _ = k.shape
    qw = Q_PER_K * blk
    head = lambda *block: pl.BlockSpec((1, 1) + block, lambda bi, h: (bi, h) + (0,) * len(block))
    cols = lambda width: pl.BlockSpec((1, s, width), lambda bi, h: (bi, 0, h))
    return pl.pallas_call(
        functools.partial(_diff_attn_kernel, blk=blk, lambda_init=lambda_init),
        grid=(b, N_HEADS),
        in_specs=[
            head(s // qw, HEAD_DIM, qw),
            cols(HEAD_DIM),
            head(s // blk, V_ROWS, blk),
            cols(HEAD_DIM),
            pl.BlockSpec((1, 1, 2 * blk), lambda bi, h: (h, 0, 0)),
            pl.BlockSpec((4, DIFF_QK), lambda bi, h: (0, 0)),
            pl.BlockSpec((1, HEAD_DIM), lambda bi, h: (0, 0)),
        ],
        out_specs=cols(HEAD_DIM),
        out_shape=jax.ShapeDtypeStruct((b, s, GROUP_WIDTH), BF16),
        scratch_shapes=[pltpu.VMEM((2, blk, blk), F32)] + _attn_scratch(blk, 2 * qw),
        compiler_params=_cparams(("arbitrary", "arbitrary")),
        name="diff_attn",
    )(qt, k, vt, gate, bias_table, lam_params, onorm)


def _out_ple_kernel(x_ref, ma_ref, mb_ref, p_ref, wo_ref, wp_ref, gn_ref, wg_ref, o_ref, h_ref):
    half = ma_ref.shape[1]
    d = o_ref.shape[1]
    chunks = [slice(lo, lo + OUT_CHUNK) for lo in range(0, d, OUT_CHUNK)]

    ma, mb = ma_ref[...], mb_ref[...]
    sum_sq = jnp.zeros((ma.shape[0], 1), F32)
    for cols in chunks:
        hc = (x_ref[:, cols] + jnp.dot(ma, wo_ref[:half, cols], preferred_element_type=F32)
              + jnp.dot(mb, wo_ref[half:, cols], preferred_element_type=F32))
        h_ref[:, cols] = hc
        sum_sq = sum_sq + jnp.sum(hc * hc, axis=-1, keepdims=True)

    hn = (h_ref[...] * lax.rsqrt(sum_sq * (1.0 / d) + EPS) * gn_ref[...]).astype(BF16)
    pb = p_ref[...].astype(BF16)
    for cols in chunks:
        zg = jnp.dot(hn, wg_ref[:, cols], preferred_element_type=F32)
        e = jnp.dot(pb, wp_ref[:, cols], preferred_element_type=F32)
        o_ref[:, cols] = h_ref[:, cols] + e * (1.0 / (1.0 + jnp.exp(-zg)))


def _out_ple(x2, mix_a, mix_b, p2, w_out, ple_proj, gate_norm, ple_gate, tm):
    m, d = x2.shape
    half = mix_a.shape[1]
    pdim = p2.shape[1]
    const = lambda i: (0, 0)
    single = dict(pipeline_mode=pl.Buffered(1))
    return pl.pallas_call(
        _out_ple_kernel,
        grid=(m // tm,),
        in_specs=[
            pl.BlockSpec((tm, d), lambda i: (i, 0)),
            pl.BlockSpec((tm, half), lambda i: (i, 0)),
            pl.BlockSpec((tm, half), lambda i: (i, 0)),
            pl.BlockSpec((tm, pdim), lambda i: (i, 0)),
            pl.BlockSpec((2 * half, d), const, **single),
            pl.BlockSpec((pdim, d), const, **single),
            pl.BlockSpec((1, d), const),
            pl.BlockSpec((d, d), const, **single),
        ],
        out_specs=pl.BlockSpec((tm, d), lambda i: (i, 0)),
        out_shape=jax.ShapeDtypeStruct((m, d), F32),
        scratch_shapes=[pltpu.VMEM((tm, d), F32)],
        compiler_params=_cparams(("arbitrary",)),
        name="out_ple",
    )(x2, mix_a, mix_b, p2, w_out, ple_proj, gate_norm, ple_gate)


def _t5_bucket(n):
    max_exact = N_BUCKETS // 2
    nf = jnp.maximum(n, 1).astype(F32)
    large = max_exact + (jnp.log(nf / max_exact) / math.log(MAX_DISTANCE / max_exact)
                         * (N_BUCKETS - max_exact)).astype(jnp.int32)
    large = jnp.minimum(large, N_BUCKETS - 1)
    return jnp.where(n < max_exact, n, large)


def _bias_table(rel_bias, blk):
    assert blk >= MAX_DISTANCE
    table = rel_bias.astype(F32)[_t5_bucket(jnp.arange(2 * blk))].T
    table = (table - rel_bias.astype(F32)[N_BUCKETS - 1][:, None]) * LOG2E
    return table[:, None, :]


def kernel(x, p, attn_norm, w_in, b_forget, fox_q_norm, fox_k_norm, diff_q_norm, diff_k_norm,
           lambda_q1, lambda_k1, lambda_q2, lambda_k2, diff_out_norm, w_out, rel_bias,
           ple_proj, ple_gate_norm, ple_gate):
    b, s, d = x.shape
    depth = w_in.shape[0]
    gw = GROUP_WIDTH
    blk, tm = KEY_BLOCK, PROJ_ROWS
    assert s % (Q_PER_K * blk) == 0 and (Q_PER_K * blk) % tm == 0 and tm % blk == 0 and tm % NORM_ROWS == 0
    bias_table = _bias_table(rel_bias, blk)
    w_t = jnp.swapaxes(w_in.astype(F32), 1, 2)

    h = x.reshape(b * s, d)
    for i in range(depth):
        lambda_init = 0.8 - 0.6 * math.exp(-0.3 * i)
        w_f = jnp.pad(w_t[i, 8 * gw:, :], ((0, LANES - N_HEADS), (0, 0)))
        b_f = jnp.pad(b_forget[i].astype(F32), (0, LANES - N_HEADS)).reshape(1, LANES)
        u, c2 = _norm_forget(h, attn_norm[i].astype(F32).reshape(1, d), w_f, b_f, s, tm)

        row = lambda a: a.astype(F32).reshape(1, -1)
        vec_spec = lambda n: pl.BlockSpec((1, n), lambda r: (0, 0))
        c_spec = pl.BlockSpec((tm, LANES), lambda r: (r, 0))
        fox_scale = HEAD_DIM ** -0.5 * LOG2E
        diff_scale = DIFF_QK ** -0.5 * LOG2E

        tiles = s // tm
        flat = lambda width, dtype: (pl.BlockSpec((tm, width), lambda r: (r, 0)),
                                     jax.ShapeDtypeStruct((b * s, width), dtype))
        qw = Q_PER_K * blk
        per_q = qw // tm
        heads_t = (pl.BlockSpec((1, N_HEADS, 1, HEAD_DIM, tm),
                                lambda r: (r // tiles, 0, (r % tiles) // per_q, 0, (r % tiles) % per_q)),
                   jax.ShapeDtypeStruct((b, N_HEADS, s // qw, HEAD_DIM, qw), BF16))
        v_t = (pl.BlockSpec((1, N_HEADS, tm // blk, V_ROWS, blk), lambda r: (r // tiles, 0, r % tiles, 0, 0)),
               jax.ShapeDtypeStruct((b, N_HEADS, s // blk, V_ROWS, blk), BF16))
        proj = lambda kern, section, extra, specs, out, name: _proj(
            kern, u, w_t, i, section, extra, specs, out[0], out[1], tm, name)
        vt_kernel = functools.partial(_proj_vt_kernel, blk=blk)

        qa = proj(functools.partial(_proj_heads_kernel, scale=fox_scale), 0,
                  [row(fox_q_norm[i])], [vec_spec(HEAD_DIM)], heads_t, "proj_fox_q")
        ka = proj(_proj_fox_key_kernel, 1, [row(fox_k_norm[i]), c2], [vec_spec(HEAD_DIM), c_spec],
                  flat(N_HEADS * AUG, BF16), "proj_fox_k")
        va = proj(vt_kernel, 2, [], [], v_t, "proj_fox_v")
        ga = proj(_proj_silu_kernel, 3, [], [], flat(gw, BF16), "proj_fox_gate")
        dq_gain = row(jnp.tile(diff_q_norm[i], 2))
        dk_gain = row(jnp.tile(diff_k_norm[i], 2))
        qb = proj(functools.partial(_proj_diff_kernel, scale=diff_scale, transposed=True), 4,
                  [dq_gain], [vec_spec(HEAD_DIM)], heads_t, "proj_diff_q")
        kb = proj(functools.partial(_proj_diff_kernel, scale=1.0, transposed=False), 5,
                  [dk_gain], [vec_spec(HEAD_DIM)], flat(gw, BF16), "proj_diff_k")
        vb = proj(vt_kernel, 6, [], [], v_t, "proj_diff_v")
        gb = proj(_proj_silu_kernel, 7, [], [], flat(gw, BF16), "proj_diff_gate")

        to3 = lambda a: a.reshape(b, s, a.shape[-1])
        c_rows = (c2[:, :N_HEADS].reshape(b, s // qw, qw, N_HEADS).transpose(0, 3, 1, 2)
                  .reshape(b, N_HEADS, s // qw, 1, qw))
        mix_a = _fox_attn(qa, c_rows, to3(ka), va, to3(ga), blk)
        lam_params = jnp.stack([lambda_q1[i], lambda_k1[i], lambda_q2[i], lambda_k2[i]]).astype(F32)
        mix_b = _diff_attn(qb, to3(kb), vb, to3(gb), bias_table, lam_params,
                           row(diff_out_norm[i]), blk, lambda_init)

        h = _out_ple(h, mix_a.reshape(b * s, gw), mix_b.reshape(b * s, gw),
                     p[i].reshape(b * s, -1), w_out[i].astype(BF16), ple_proj[i].astype(BF16),
                     row(ple_gate_norm[i]), ple_gate[i].astype(BF16), OUT_ROWS)
    return h.reshape(b, s, d)
```
